```python
import math
import jax, jax.numpy as jnp
from jax import lax
import numpy as np

D_MODEL = 2048
BATCH = 8
SEQ = 2048
DEPTH = 2
DEC_BATCH = 128
DEC_SEQ = 4
PAST_LEN = 16384
PAGE_SIZE = 128

MLA_HEADS = 16
Q_LORA = 512
KV_LORA = 512
NOPE_DIM = 128
ROPE_DIM = 64
QK_DIM = NOPE_DIM + ROPE_DIM
V_DIM = 128
ROPE_THETA = 10000.0
ATTN_SCALE = QK_DIM ** -0.5
Q_BLOCK = 128
CONV_CH = 1024
CONV_K = 31
HGRN_HEADS = 8
HGRN_DK = 128
HGRN_DV = 128
HGRN_KW = HGRN_HEADS * HGRN_DK
HGRN_VW = HGRN_HEADS * HGRN_DV
HGRN_CHUNK = 32
LB_FLOOR = 1e-30
D_FF = ((8 * D_MODEL // 3 + 255) // 256) * 256
PLE_DIM = 256
N_BRANCH = 3
EPS = 1e-6
NEG_BIG = -1e30

IN_SIZES = (Q_LORA, KV_LORA, ROPE_DIM, 2 * CONV_CH, HGRN_KW, HGRN_KW, HGRN_VW, HGRN_VW, N_BRANCH * D_MODEL)
IN_COLS = sum(IN_SIZES)
SPLIT_POINTS = tuple(int(s) for s in np.cumsum(IN_SIZES)[:-1])

kernel_name = 'hybrid_mla_conformer_hgrn2_decode_step'


def _softmax(s, axis=-1):
    s = s - lax.stop_gradient(jnp.max(s, axis=axis, keepdims=True))
    e = jnp.exp(s)
    return e / jnp.sum(e, axis=axis, keepdims=True)


def _rms(x, g):
    xf = x.astype(jnp.float32)
    y = xf * lax.rsqrt(jnp.mean(xf * xf, axis=-1, keepdims=True) + EPS)
    return (y * g.astype(jnp.float32)).astype(x.dtype)


def _layernorm(x, g, b):
    xf = x.astype(jnp.float32)
    xc = xf - jnp.mean(xf, axis=-1, keepdims=True)
    var = jnp.mean(xc * xc, axis=-1, keepdims=True)
    return (xc * lax.rsqrt(var + EPS) * g.astype(jnp.float32) + b.astype(jnp.float32)).astype(x.dtype)


def _rope_tables(pos):
    half = ROPE_DIM // 2
    inv = ROPE_THETA ** (-jnp.arange(half, dtype=jnp.float32) / half)
    ang = pos.astype(jnp.float32)[:, None] * inv[None, :]
    return jnp.cos(ang), jnp.sin(ang)


def _rope(x, cos, sin):
    half = ROPE_DIM // 2
    shape = (cos.shape[0],) + (1,) * (x.ndim - 3) + (half,)
    c = cos.reshape(shape)
    s = sin.reshape(shape)
    xf = x.astype(jnp.float32)
    x1, x2 = xf[..., :half], xf[..., half:]
    return jnp.concatenate([x1 * c - x2 * s, x2 * c + x1 * s], axis=-1).astype(x.dtype)


def _mla_qk(cq, ckv, kr, cos, sin, g_q, w_uq, g_kv, g_qk, g_kk, w_uk):
    b, l = cq.shape[:2]
    q = jnp.einsum('blr,rn->bln', _rms(cq, g_q), w_uq).reshape(b, l, MLA_HEADS, QK_DIM)
    q = _rms(q, g_qk)
    q_nope = q[..., :NOPE_DIM] * g_kk[:NOPE_DIM]
    q_pe = _rope(q[..., NOPE_DIM:], cos, sin)
    c = _rms(ckv, g_kv)
    k_nope = jnp.einsum('blc,chd->blhd', c, w_uk)
    sq = (jnp.sum(jnp.square(k_nope.astype(jnp.float32)), axis=-1)
          + jnp.sum(jnp.square(kr.astype(jnp.float32)), axis=-1, keepdims=True))
    kscale = lax.rsqrt(sq / QK_DIM + EPS)
    k_pe = _rope(kr * g_kk[NOPE_DIM:], cos, sin)
    return q_nope, q_pe, c, k_nope, k_pe, kscale


def _mla_prompt(q_nope, q_pe, c, k_nope, k_pe, kscale, w_uv):
    b, l = q_nope.shape[:2]
    dt = q_nope.dtype
    kn = (k_nope * kscale[..., None]).astype(dt)
    kp = (k_pe[:, :, None, :] * kscale[..., None]).astype(dt)
    v = jnp.einsum('blc,chd->blhd', c, w_uv)
    qb = min(Q_BLOCK, l)
    nb = l // qb

    def blocks(a):
        return a.reshape(b, nb, qb, MLA_HEADS, a.shape[-1]).transpose(1, 0, 2, 3, 4)

    kpos = jnp.arange(l)

    def one_block(args):
        qn, qp, start = args
        s = (jnp.einsum('bqhd,bkhd->bhqk', qn, kn)
             + jnp.einsum('bqhr,bkhr->bhqk', qp, kp)).astype(jnp.float32) * ATTN_SCALE
        qpos = start + jnp.arange(qb)
        s = jnp.where(kpos[None, :] <= qpos[:, None], s, NEG_BIG)
        p = _softmax(s, axis=-1).astype(v.dtype)
        return jnp.einsum('bhqk,bkhd->bqhd', p, v)

    o = lax.map(one_block, (blocks(q_nope), blocks(q_pe), jnp.arange(nb) * qb))
    return o.transpose(1, 0, 2, 3, 4).reshape(b, l, MLA_HEADS * V_DIM)


def _mla_sample(q_nope, q_pe, c, k_pe, kscale, cache_latent, cache_kscale, page_table, layer, w_uk, w_uv):
    b, l = q_nope.shape[:2]
    f32 = jnp.float32
    qa = jnp.einsum('blhd,chd->bhlc', q_nope, w_uk).astype(f32)
    qp = q_pe.transpose(0, 2, 1, 3).astype(f32)

    def page_step(carry, phys):
        m, den, acc = carry
        rows = cache_latent[layer, phys].astype(f32)
        ks = cache_kscale[layer, phys].astype(f32)
        lat = rows[..., :KV_LORA]
        s = (jnp.einsum('bhlc,bpc->bhlp', qa, lat)
             + jnp.einsum('bhlr,bpr->bhlp', qp, rows[..., KV_LORA:]))
        s = s * ks.transpose(0, 2, 1)[:, :, None, :] * ATTN_SCALE
        m_new = jnp.maximum(m, jnp.max(s, axis=-1))
        pexp = jnp.exp(s - m_new[..., None])
        corr = jnp.exp(m - m_new)
        den = den * corr + jnp.sum(pexp, axis=-1)
        acc = acc * corr[..., None] + jnp.einsum('bhlp,bpc->bhlc', pexp, lat)
        return (m_new, den, acc), None

    init = (jnp.full((b, MLA_HEADS, l), NEG_BIG, f32),
            jnp.zeros((b, MLA_HEADS, l), f32),
            jnp.zeros((b, MLA_HEADS, l, KV_LORA), f32))
    (m, den, acc), _ = lax.scan(page_step, init, page_table.T)
    cf = c.astype(f32)
    s = (jnp.einsum('bhlc,bmc->bhlm', qa, cf)
         + jnp.einsum('bhlr,bmr->bhlm', qp, k_pe.astype(f32)))
    s = s * kscale.astype(f32).transpose(0, 2, 1)[:, :, None, :] * ATTN_SCALE
    s = jnp.where(jnp.tril(jnp.ones((l, l), bool)), s, NEG_BIG)
    m_new = jnp.maximum(m, jnp.max(s, axis=-1))
    pexp = jnp.exp(s - m_new[..., None])
    corr = jnp.exp(m - m_new)
    den = den * corr + jnp.sum(pexp, axis=-1)
    acc = acc * corr[..., None] + jnp.einsum('bhlm,bmc->bhlc', pexp, cf)
    o_lat = (acc / den[..., None]).astype(q_nope.dtype)
    o = jnp.einsum('bhlc,chd->blhd', o_lat, w_uv)
    return o.reshape(b, l, MLA_HEADS * V_DIM)


def _conv_branch(u, buf, w_dw, b_dw, g_ln, b_ln, w_pw):
    a = u[..., :CONV_CH] * jax.nn.sigmoid(u[..., CONV_CH:])
    full = jnp.concatenate([buf.astype(a.dtype), a], axis=1)
    y = lax.conv_general_dilated(full, w_dw[:, None, :].astype(full.dtype), window_strides=(1,),
                                 padding='VALID', dimension_numbers=('NWC', 'WIO', 'NWC'),
                                 feature_group_count=CONV_CH) + b_dw
    y = jax.nn.silu(_layernorm(y, g_ln, b_ln))
    return jnp.einsum('blc,cd->bld', y, w_pw), full[:, full.shape[1] - (CONV_K - 1):]


def _chunk_recurrence(q, k, v, logf, s0):
    b, l, h, dk = q.shape
    dv = v.shape[-1]
    cs = HGRN_CHUNK if l % HGRN_CHUNK == 0 else l
    n = l // cs

    def to_chunks(a):
        return a.reshape(b, n, cs, h, a.shape[-1]).transpose(1, 0, 3, 2, 4)

    tri = jnp.tril(jnp.ones((cs, cs), bool))

    def step(s, xs):
        qc, kc, vc, gc = xs
        cum = jnp.cumsum(gc, axis=2)
        inter = jnp.einsum('bhtk,bhkv->bhtv', qc * jnp.exp(cum), s)
        diff = jnp.where(tri[:, :, None], cum[:, :, :, None, :] - cum[:, :, None, :, :], NEG_BIG)
        att = jnp.einsum('bhtk,bhtsk,bhsk->bhts', qc, jnp.exp(diff), kc)
        o = inter + jnp.einsum('bhts,bhsv->bhtv', att, vc)
        last = cum[:, :, -1, :]
        s = (jnp.exp(last)[..., None] * s
             + jnp.einsum('bhsk,bhsv->bhkv', kc * jnp.exp(last[:, :, None, :] - cum), vc))
        return s, o

    s, o = lax.scan(step, s0, (to_chunks(q), to_chunks(k), to_chunks(v), to_chunks(logf)))
    return o.transpose(1, 0, 3, 2, 4).reshape(b, l, h, dv), s


def _hgrn2(hq, hf, hi, hg, lb, s0, g_o):
    b, l = hq.shape[:2]
    f32 = jnp.float32
    q = jax.nn.silu(hq.astype(f32)).reshape(b, l, HGRN_HEADS, HGRN_DK)
    lbh = lb.reshape(HGRN_HEADS, HGRN_DK).astype(f32)
    logf = jnp.logaddexp(jnp.log(jnp.maximum(lbh, LB_FLOOR)), jnp.log1p(-lbh)
                         + jax.nn.log_sigmoid(hf.astype(f32).reshape(b, l, HGRN_HEADS, HGRN_DK)))
    k = -jnp.expm1(logf)
    v = jax.nn.silu(hi.astype(f32)).reshape(b, l, HGRN_HEADS, HGRN_DV)
    o, s = _chunk_recurrence(q, k, v, logf, s0.astype(f32))
    o = _rms(o, g_o) * jax.nn.sigmoid(hg.astype(f32)).reshape(b, l, HGRN_HEADS, HGRN_DV)
    return o.reshape(b, l, HGRN_VW).astype(hq.dtype), s


def _layer(x, p_emb, pos, lb, conv_buf, s0, w, paged):
    b, l, _ = x.shape
    dt = x.dtype
    h = _rms(x, w['g_mix'])
    z = jnp.einsum('bld,dn->bln', h, w['w_in'])
    cq, ckv, kr, u_conv, hq, hf, hi, hg, gates = jnp.split(z, SPLIT_POINTS, axis=-1)
    cos, sin = _rope_tables(pos)
    q_nope, q_pe, c, k_nope, k_pe, kscale = _mla_qk(cq, ckv, kr, cos, sin, w['g_q'], w['w_uq'], w['g_kv'],
                                                    w['g_qk'], w['g_kk'], w['w_uk'])
    if paged is None:
        o_mla = _mla_prompt(q_nope, q_pe, c, k_nope, k_pe, kscale, w['w_uv'])
    else:
        cache_latent, cache_kscale, page_table, layer = paged
        o_mla = _mla_sample(q_nope, q_pe, c, k_pe, kscale, cache_latent, cache_kscale, page_table, layer,
                            w['w_uk'], w['w_uv'])
    y_mla = jnp.einsum('bln,nd->bld', o_mla, w['w_o_mla'])
    y_conv, conv_new = _conv_branch(u_conv, conv_buf, w['w_dw'], w['b_dw'], w['g_cln'], w['b_cln'], w['w_pw'])
    o_h, s_new = _hgrn2(hq, hf, hi, hg, lb, s0, w['g_hgrn'])
    y_hgrn = jnp.einsum('bln,nd->bld', o_h, w['w_o_hgrn'])
    g = jax.nn.sigmoid(gates.astype(jnp.float32)).reshape(b, l, N_BRANCH, D_MODEL)
    mix = (g[:, :, 0] * y_mla + g[:, :, 1] * y_conv + g[:, :, 2] * y_hgrn).astype(dt)
    x = x + jnp.einsum('bld,de->ble', mix, w['w_out']).astype(dt)
    gu = jnp.einsum('bld,df->blf', _rms(x, w['g_ffn']), w['w_gu'])
    x = x + jnp.einsum('blf,fd->bld', jax.nn.silu(gu[..., :D_FF]) * gu[..., D_FF:], w['w_down']).astype(dt)
    gate = jax.nn.sigmoid(jnp.einsum('bld,de->ble', _rms(x, w['g_ple']), w['w_ple_gate']))
    x = x + (jnp.einsum('blp,pd->bld', p_emb, w['w_ple_proj']) * gate).astype(dt)
    rows = jnp.concatenate([c, k_pe.astype(c.dtype)], axis=-1)
    return x, (rows.astype(dt), kscale.astype(dt), conv_new.astype(dt), s_new.astype(dt))


def setup_inputs(seed: int = 0) -> dict:
    key = jax.random.key(seed)
    ks = jax.random.split(key, 40)
    f32 = jnp.float32
    n_pages = PAST_LEN // PAGE_SIZE
    n_pool = (5 * DEC_BATCH * n_pages) // 4

    def nrm(k, shape, scale):
        return jax.random.normal(k, shape, f32) * scale

    def gain(k, shape):
        return 1.0 + 0.1 * jax.random.normal(k, shape, f32)

    page_table = jax.random.permutation(ks[6], n_pool)[: DEC_BATCH * n_pages]
    page_table = page_table.reshape(DEC_BATCH, n_pages).astype(jnp.int32)
    return {
        'x_prompt': nrm(ks[0], (BATCH, SEQ, D_MODEL), 1.0),
        'x_sample': nrm(ks[1], (DEC_BATCH, DEC_SEQ, D_MODEL), 1.0),
        'cache_latent': nrm(ks[2], (DEPTH, n_pool, PAGE_SIZE, KV_LORA + ROPE_DIM), 1.0),
        'cache_kscale': jax.random.uniform(ks[3], (DEPTH, n_pool, PAGE_SIZE, MLA_HEADS), f32, 0.5, 1.5),
        'state_conv': nrm(ks[4], (DEPTH, DEC_BATCH, CONV_K - 1, CONV_CH), 0.5),
        'state_hgrn': nrm(ks[5], (DEPTH, DEC_BATCH, HGRN_HEADS, HGRN_DK, HGRN_DV), 0.5),
        'page_table': page_table,
        'p_prompt': nrm(ks[7], (DEPTH, BATCH, SEQ, PLE_DIM), 1.0),
        'p_sample': nrm(ks[8], (DEPTH, DEC_BATCH, DEC_SEQ, PLE_DIM), 1.0),
        'g_mix': gain(ks[9], (DEPTH, D_MODEL)),
        'w_in': nrm(ks[10], (DEPTH, D_MODEL, IN_COLS), D_MODEL ** -0.5),
        'g_q': gain(ks[11], (DEPTH, Q_LORA)),
        'w_uq': nrm(ks[12], (DEPTH, Q_LORA, MLA_HEADS * QK_DIM), Q_LORA ** -0.5),
        'g_kv': gain(ks[13], (DEPTH, KV_LORA)),
        'g_qk': gain(ks[14], (DEPTH, QK_DIM)),
        'g_kk': gain(ks[15], (DEPTH, QK_DIM)),
        'w_uk': nrm(ks[16], (DEPTH, KV_LORA, MLA_HEADS, NOPE_DIM), KV_LORA ** -0.5),
        'w_uv': nrm(ks[17], (DEPTH, KV_LORA, MLA_HEADS, V_DIM), KV_LORA ** -0.5),
        'w_o_mla': nrm(ks[18], (DEPTH, MLA_HEADS * V_DIM, D_MODEL), (MLA_HEADS * V_DIM) ** -0.5),
        'w_dw': nrm(ks[19], (DEPTH, CONV_K, CONV_CH), CONV_K ** -0.5),
        'b_dw': nrm(ks[20], (DEPTH, CONV_CH), 0.02),
        'g_cln': gain(ks[21], (DEPTH, CONV_CH)),
        'b_cln': nrm(ks[22], (DEPTH, CONV_CH), 0.02),
        'w_pw': nrm(ks[23], (DEPTH, CONV_CH, D_MODEL), CONV_CH ** -0.5),
        'lb_logits': nrm(ks[24], (DEPTH, HGRN_KW), 0.5),
        'g_hgrn': gain(ks[25], (DEPTH, HGRN_DV)),
        'w_o_hgrn': nrm(ks[26], (DEPTH, HGRN_VW, D_MODEL), HGRN_VW ** -0.5),
        'w_out': nrm(ks[27], (DEPTH, D_MODEL, D_MODEL), D_MODEL ** -0.5),
        'g_ffn': gain(ks[28], (DEPTH, D_MODEL)),
        'w_gu': nrm(ks[29], (DEPTH, D_MODEL, 2 * D_FF), D_MODEL ** -0.5),
        'w_down': nrm(ks[30], (DEPTH, D_FF, D_MODEL), D_FF ** -0.5),
        'g_ple': gain(ks[31], (DEPTH, D_MODEL)),
        'w_ple_gate': nrm(ks[32], (DEPTH, D_MODEL, D_MODEL), D_MODEL ** -0.5),
        'w_ple_proj': nrm(ks[33], (DEPTH, PLE_DIM, D_MODEL), PLE_DIM ** -0.5),
    }


def reference(x_prompt, x_sample, cache_latent, cache_kscale, state_conv, state_hgrn, page_table,
              p_prompt, p_sample, g_mix, w_in, g_q, w_uq, g_kv, g_qk, g_kk, w_uk, w_uv, w_o_mla,
              w_dw, b_dw, g_cln, b_cln, w_pw, lb_logits, g_hgrn, w_o_hgrn, w_out, g_ffn, w_gu, w_down,
              g_ple, w_ple_gate, w_ple_proj):
    f32 = jnp.float32
    lb_p = _softmax(lb_logits.astype(f32), axis=0)
    lb_all = jnp.cumsum(lb_p, axis=0) - lb_p[0:1]
    bp, lp = x_prompt.shape[:2]
    ls = x_sample.shape[1]
    past = page_table.shape[1] * cache_latent.shape[2]
    pos_p = jnp.arange(lp, dtype=jnp.int32)
    pos_s = past + jnp.arange(ls, dtype=jnp.int32)
    xp, xs = x_prompt, x_sample
    outs_p, outs_s = [], []
    for i in range(DEPTH):
        w = dict(g_mix=g_mix[i], w_in=w_in[i], g_q=g_q[i], w_uq=w_uq[i], g_kv=g_kv[i], g_qk=g_qk[i],
                 g_kk=g_kk[i], w_uk=w_uk[i], w_uv=w_uv[i], w_o_mla=w_o_mla[i], w_dw=w_dw[i], b_dw=b_dw[i],
                 g_cln=g_cln[i], b_cln=b_cln[i], w_pw=w_pw[i], g_hgrn=g_hgrn[i], w_o_hgrn=w_o_hgrn[i],
                 w_out=w_out[i], g_ffn=g_ffn[i], w_gu=w_gu[i], w_down=w_down[i], g_ple=g_ple[i],
                 w_ple_gate=w_ple_gate[i], w_ple_proj=w_ple_proj[i])
        xp, st_p = _layer(xp, p_prompt[i], pos_p, lb_all[i],
                          jnp.zeros((bp, CONV_K - 1, CONV_CH), xp.dtype),
                          jnp.zeros((bp, HGRN_HEADS, HGRN_DK, HGRN_DV), f32), w, None)
        xs, st_s = _layer(xs, p_sample[i], pos_s, lb_all[i], state_conv[i], state_hgrn[i], w,
                          (cache_latent, cache_kscale, page_table, i))
        outs_p.append(st_p)
        outs_s.append(st_s)
    lat_p = jnp.stack([o[0] for o in outs_p])
    ks_p = jnp.stack([o[1] for o in outs_p])
    conv_p = jnp.stack([o[2] for o in outs_p])
    hg_p = jnp.stack([o[3] for o in outs_p])
    lat_s = jnp.stack([o[0] for o in outs_s])
    ks_s = jnp.stack([o[1] for o in outs_s])
    conv_s = jnp.stack([o[2] for o in outs_s])
    hg_s = jnp.stack([o[3] for o in outs_s])
    return (xp, xs, lat_p, ks_p, conv_p, hg_p, lat_s, ks_s, conv_s, hg_s)
```

```python
import functools

import jax
import jax.numpy as jnp
from jax import lax
from jax.experimental import pallas as pl
from jax.experimental.pallas import tpu as pltpu

F32 = jnp.float32
BF16 = jnp.bfloat16

D_MODEL = 2048
MLA_HEADS = 16
Q_LORA = 512
KV_LORA = 512
NOPE_DIM = 128
ROPE_DIM = 64
QK_DIM = NOPE_DIM + ROPE_DIM
V_DIM = 128
ROPE_THETA = 10000.0
ATTN_SCALE = QK_DIM ** -0.5
CONV_CH = 1024
CONV_K = 31
HGRN_HEADS = 8
HGRN_DK = 128
HGRN_DV = 128
HGRN_KW = HGRN_HEADS * HGRN_DK
HGRN_CHUNK = 32
LB_FLOOR = 1e-30
D_FF = ((8 * D_MODEL // 3 + 255) // 256) * 256
PLE_DIM = 256
EPS = 1e-6
NEG_BIG = -1e30

LANES = 128
SUBLANES = 8
VMEM_LIMIT_BYTES = 56 * 1024 * 1024

Z_CQ = 0
Z_CKV = 512
Z_CONV_A = 1024
Z_CONV_G = 2048
Z_HQ = 3072
Z_HF = 4096
Z_HI = 5120
Z_HG = 6144
Z_G0 = 7168
Z_G1 = 9216
Z_G2 = 11264
Z_KR = 13312
Z_COLS = 13824


def _pick(total, target, mult=SUBLANES):
    best = None
    for d in range(mult, min(total, target) + 1, mult):
        if total % d == 0:
            best = d
    assert best is not None, (total, target, mult)
    return best


def _params(*sem):
    return pltpu.CompilerParams(dimension_semantics=sem, vmem_limit_bytes=VMEM_LIMIT_BYTES)


def _rms_rows(x, g):
    ms = jnp.mean(x * x, axis=-1, keepdims=True)
    return x * lax.rsqrt(ms + EPS) * g


def _sigmoid(x):
    return 1.0 / (1.0 + jnp.exp(-x))


def _silu(x):
    return x * _sigmoid(x)


def _rope128(x, cos_t, sin_t):
    lane = lax.broadcasted_iota(jnp.int32, x.shape, 1)
    first_half = (lane & (ROPE_DIM // 2)) == 0
    partner = jnp.where(first_half, pltpu.roll(x, LANES - ROPE_DIM // 2, 1), pltpu.roll(x, ROPE_DIM // 2, 1))
    return x * cos_t + partner * sin_t


def _inproj_kernel(x_ref, g_ref, w_ref, z_ref, h_scr):
    @pl.when(pl.program_id(1) == 0)
    def _():
        h_scr[...] = _rms_rows(x_ref[...], g_ref[...]).astype(BF16)

    z_ref[...] = jnp.dot(h_scr[...], w_ref[...], preferred_element_type=F32)


def _inproj(x, g, w):
    m, k = x.shape
    n = w.shape[1]
    bm = _pick(m, 1536)
    bn = _pick(n, 768, LANES)
    return pl.pallas_call(
        _inproj_kernel,
        grid=(m // bm, n // bn),
        in_specs=[pl.BlockSpec((bm, k), lambda i, j: (i, 0)),
                  pl.BlockSpec((1, k), lambda i, j: (0, 0)),
                  pl.BlockSpec((k, bn), lambda i, j: (0, j))],
        out_specs=pl.BlockSpec((bm, bn), lambda i, j: (i, j)),
        out_shape=jax.ShapeDtypeStruct((m, n), F32),
        scratch_shapes=[pltpu.VMEM((bm, k), BF16)],
        compiler_params=_params("parallel", "arbitrary"),
        name="inproj",
    )(x, g, w)


def _qproj_kernel(cq_ref, gq_ref, w_ref, gqkn_ref, gkkn_ref, gqkr_ref, cos_ref, sin_ref, q_ref):
    h = _rms_rows(cq_ref[...], gq_ref[...]).astype(BF16)
    acc = jnp.dot(h, w_ref[...], preferred_element_type=F32)
    cos_t = cos_ref[...]
    sin_t = sin_ref[...]
    lane = lax.broadcasted_iota(jnp.int32, cos_t.shape, 1)
    low = lane < ROPE_DIM
    nope_w = MLA_HEADS * NOPE_DIM
    for p in range(MLA_HEADS // 2):
        r = acc[:, nope_w + LANES * p: nope_w + LANES * (p + 1)]
        r2 = r * r
        ssr0 = jnp.sum(jnp.where(low, r2, 0.0), axis=-1, keepdims=True)
        ssr1 = jnp.sum(jnp.where(low, 0.0, r2), axis=-1, keepdims=True)
        invs = []
        for u, ssr in ((0, ssr0), (1, ssr1)):
            hd = 2 * p + u
            nq = acc[:, NOPE_DIM * hd: NOPE_DIM * (hd + 1)]
            ss = jnp.sum(nq * nq, axis=-1, keepdims=True) + ssr
            inv = lax.rsqrt(ss / QK_DIM + EPS)
            invs.append(inv)
            qn = nq * inv * gqkn_ref[...] * gkkn_ref[...]
            q_ref[hd, :, 0:NOPE_DIM] = (qn * ATTN_SCALE).astype(BF16)
        rn = r * jnp.where(low, invs[0], invs[1]) * gqkr_ref[...]
        rr = (_rope128(rn, cos_t, sin_t) * ATTN_SCALE).astype(BF16)
        q_ref[2 * p, :, NOPE_DIM:QK_DIM] = rr[:, 0:ROPE_DIM]
        q_ref[2 * p + 1, :, NOPE_DIM:QK_DIM] = rr[:, ROPE_DIM:LANES]


def _qproj(z, g_q, w_uq_r, gqkn, gkkn, gqkr, cos_t, sin_t):
    m = z.shape[0]
    bm = _pick(m, 512, 16)
    n = w_uq_r.shape[1]
    return pl.pallas_call(
        _qproj_kernel,
        grid=(m // bm,),
        in_specs=[pl.BlockSpec((bm, Q_LORA), lambda i: (i, Z_CQ // Q_LORA)),
                  pl.BlockSpec((1, Q_LORA), lambda i: (0, 0)),
                  pl.BlockSpec((Q_LORA, n), lambda i: (0, 0)),
                  pl.BlockSpec((1, NOPE_DIM), lambda i: (0, 0)),
                  pl.BlockSpec((1, NOPE_DIM), lambda i: (0, 0)),
                  pl.BlockSpec((1, LANES), lambda i: (0, 0)),
                  pl.BlockSpec((bm, LANES), lambda i: (i, 0)),
                  pl.BlockSpec((bm, LANES), lambda i: (i, 0))],
        out_specs=pl.BlockSpec((MLA_HEADS, bm, QK_DIM), lambda i: (0, i, 0)),
        out_shape=jax.ShapeDtypeStruct((MLA_HEADS, m, QK_DIM), BF16),
        compiler_params=_params("parallel"),
        name="qproj",
    )(z, g_q, w_uq_r, gqkn, gkkn, gqkr, cos_t, sin_t)


def _kvproj_kernel(ckv_ref, kr_ref, gkv_ref, gkkr_ref, wuk_ref, wuv_ref, cos_ref, sin_ref,
                   rows_ref, ks_ref, k_ref, v_ref):
    c = _rms_rows(ckv_ref[...], gkv_ref[...])
    rows_ref[:, 0:KV_LORA] = c
    cb = c.astype(BF16)
    kn = jnp.dot(cb, wuk_ref[...], preferred_element_type=F32)
    v = jnp.dot(cb, wuv_ref[...], preferred_element_type=F32)
    kr = kr_ref[...]
    ss_kr = jnp.sum(kr * kr, axis=-1, keepdims=True)
    kpe = _rope128(kr * gkkr_ref[...], cos_ref[...], sin_ref[...])
    rows_ref[:, KV_LORA:KV_LORA + ROPE_DIM] = kpe[:, 0:ROPE_DIM]
    lane = lax.broadcasted_iota(jnp.int32, kr.shape, 1)
    ks_all = jnp.zeros(kr.shape, F32)
    for hd in range(MLA_HEADS):
        knh = kn[:, NOPE_DIM * hd: NOPE_DIM * (hd + 1)]
        ks = lax.rsqrt((jnp.sum(knh * knh, axis=-1, keepdims=True) + ss_kr) / QK_DIM + EPS)
        k_ref[hd, :, 0:NOPE_DIM] = (knh * ks).astype(BF16)
        k_ref[hd, :, NOPE_DIM:QK_DIM] = (kpe * ks).astype(BF16)[:, 0:ROPE_DIM]
        v_ref[hd] = v[:, V_DIM * hd: V_DIM * (hd + 1)].astype(BF16)
        ks_all = jnp.where(lane == hd, ks, ks_all)
    ks_ref[...] = ks_all[:, 0:MLA_HEADS]


def _kvproj(z, g_kv, gkkr, w_uk2, w_uv2, cos_t, sin_t):
    m = z.shape[0]
    bm = _pick(m, 512, 16)
    return pl.pallas_call(
        _kvproj_kernel,
        grid=(m // bm,),
        in_specs=[pl.BlockSpec((bm, KV_LORA), lambda i: (i, Z_CKV // KV_LORA)),
                  pl.BlockSpec((bm, LANES), lambda i: (i, Z_KR // LANES)),
                  pl.BlockSpec((1, KV_LORA), lambda i: (0, 0)),
                  pl.BlockSpec((1, LANES), lambda i: (0, 0)),
                  pl.BlockSpec((KV_LORA, MLA_HEADS * NOPE_DIM), lambda i: (0, 0)),
                  pl.BlockSpec((KV_LORA, MLA_HEADS * V_DIM), lambda i: (0, 0)),
                  pl.BlockSpec((bm, LANES), lambda i: (i, 0)),
                  pl.BlockSpec((bm, LANES), lambda i: (i, 0))],
        out_specs=[pl.BlockSpec((bm, KV_LORA + ROPE_DIM), lambda i: (i, 0)),
                   pl.BlockSpec((bm, MLA_HEADS), lambda i: (i, 0)),
                   pl.BlockSpec((MLA_HEADS, bm, QK_DIM), lambda i: (0, i, 0)),
                   pl.BlockSpec((MLA_HEADS, bm, V_DIM), lambda i: (0, i, 0))],
        out_shape=[jax.ShapeDtypeStruct((m, KV_LORA + ROPE_DIM), F32),
                   jax.ShapeDtypeStruct((m, MLA_HEADS), F32),
                   jax.ShapeDtypeStruct((MLA_HEADS, m, QK_DIM), BF16),
                   jax.ShapeDtypeStruct((MLA_HEADS, m, V_DIM), BF16)],
        compiler_params=_params("parallel"),
        name="kvproj",
    )(z, z, g_kv, gkkr, w_uk2, w_uv2, cos_t, sin_t)


def _flash_kernel(q_ref, k_ref, v_ref, o_ref, *, tq):
    qi = pl.program_id(2)
    q = q_ref[...]

    def step(kb, carry, masked):
        m, l, acc = carry
        r0 = pl.multiple_of(kb * tq, tq)
        k = k_ref[pl.ds(r0, tq), :]
        v = v_ref[pl.ds(r0, tq), :]
        s = lax.dot_general(q, k, (((1,), (1,)), ((), ())), preferred_element_type=F32)
        if masked:
            row = lax.broadcasted_iota(jnp.int32, s.shape, 0)
            col = lax.broadcasted_iota(jnp.int32, s.shape, 1)
            s = jnp.where(col <= row, s, NEG_BIG)
        m_new = jnp.maximum(m, jnp.max(s, axis=-1, keepdims=True))
        p = jnp.exp(s - m_new)
        corr = jnp.exp(m - m_new)
        l = l * corr + jnp.sum(p, axis=-1, keepdims=True)
        acc = acc * corr + jnp.dot(p.astype(BF16), v, preferred_element_type=F32)
        return m_new, l, acc

    init = (jnp.full((tq, 1), NEG_BIG, F32), jnp.zeros((tq, 1), F32), jnp.zeros((tq, V_DIM), F32))
    carry = lax.fori_loop(0, qi, lambda kb, c: step(kb, c, False), init)
    m, l, acc = step(qi, carry, True)
    o_ref[...] = (acc / l).astype(o_ref.dtype)


def _flash(q, k, v, batch, seq):
    tq = _pick(seq, 512, 16)
    nq = seq // tq
    mp = batch * seq
    return pl.pallas_call(
        functools.partial(_flash_kernel, tq=tq),
        grid=(batch, MLA_HEADS, nq),
        in_specs=[pl.BlockSpec((None, tq, QK_DIM), lambda b, h, i: (h, b * nq + i, 0)),
                  pl.BlockSpec((None, seq, QK_DIM), lambda b, h, i: (h, b, 0)),
                  pl.BlockSpec((None, seq, V_DIM), lambda b, h, i: (h, b, 0))],
        out_specs=pl.BlockSpec((tq, V_DIM), lambda b, h, i: (b * nq + i, h)),
        out_shape=jax.ShapeDtypeStruct((mp, MLA_HEADS * V_DIM), BF16),
        compiler_params=_params("parallel", "parallel", "arbitrary"),
        name="flash_prompt",
    )(q, k, v)


def _absorb_kernel(q_ref, w_ref, o_ref):
    q = q_ref[...]
    qa = jnp.dot(q[:, 0:NOPE_DIM], w_ref[...], preferred_element_type=F32)
    o_ref[:, 0:KV_LORA] = qa.astype(BF16)
    o_ref[:, KV_LORA:KV_LORA + ROPE_DIM] = q[:, NOPE_DIM:QK_DIM]


def _absorb(q, w_uk_t, mp, ms):
    assert mp % ms == 0
    return pl.pallas_call(
        _absorb_kernel,
        grid=(MLA_HEADS,),
        in_specs=[pl.BlockSpec((None, ms, QK_DIM), lambda h: (h, mp // ms, 0)),
                  pl.BlockSpec((None, NOPE_DIM, KV_LORA), lambda h: (h, 0, 0))],
        out_specs=pl.BlockSpec((None, ms, KV_LORA + ROPE_DIM), lambda h: (h, 0, 0)),
        out_shape=jax.ShapeDtypeStruct((MLA_HEADS, ms, KV_LORA + ROPE_DIM), BF16),
        compiler_params=_params("parallel"),
        name="absorb_q",
    )(q, w_uk_t)


def _split3(x):
    hi = x.astype(BF16)
    r1 = x - hi.astype(F32)
    mid = r1.astype(BF16)
    lo = (r1 - mid.astype(F32)).astype(BF16)
    return hi, mid, lo


def _paged_kernel(pt_ref, q_ref, e_ref, *refs, pages_per_step, n_chunks, dec_seq, page):
    g = pages_per_step
    lat_refs = refs[0:g]
    ksc_refs = refs[g:2 * g]
    new_rows_ref, new_ks_ref, o_ref, m_scr, l_scr, acc_scr, tail_lat, tail_ks = refs[2 * g:]
    b = pl.program_id(0)
    c = pl.program_id(1)
    rows = q_ref.shape[0]

    @pl.when(jnp.logical_and(b == 0, c == 0))
    def _():
        tail_lat[...] = jnp.zeros(tail_lat.shape, F32)
        tail_ks[...] = jnp.zeros(tail_ks.shape, F32)

    @pl.when(c == 0)
    def _():
        m_scr[...] = jnp.full(m_scr.shape, NEG_BIG, F32)
        l_scr[...] = jnp.zeros(l_scr.shape, F32)
        acc_scr[...] = jnp.zeros(acc_scr.shape, F32)

    q = q_ref[...]
    e = e_ref[...]

    def scale_rows(ks):
        out = None
        for part in _split3(ks):
            t = lax.dot_general(e, part, (((1,), (1,)), ((), ())), preferred_element_type=F32)
            out = t if out is None else out + t
        return out

    def update(lat, ks, valid):
        latb = lat.astype(BF16)
        s = lax.dot_general(q, latb, (((1,), (1,)), ((), ())), preferred_element_type=F32)
        s = s * scale_rows(ks)
        if valid is not None:
            s = jnp.where(valid, s, NEG_BIG)
        m = m_scr[...]
        m_new = jnp.maximum(m, jnp.max(s, axis=-1, keepdims=True))
        p = jnp.exp(s - m_new[:, 0:1])
        corr = jnp.exp(m - m_new)
        l_scr[...] = l_scr[...] * corr + jnp.sum(p, axis=-1, keepdims=True)
        acc_scr[...] = acc_scr[...] * corr[:, 0:1] + jnp.dot(p.astype(BF16), latb[:, 0:KV_LORA],
                                                             preferred_element_type=F32)
        m_scr[...] = m_new

    lat = jnp.concatenate([r[...] for r in lat_refs], axis=0)
    ks = jnp.concatenate([r[...] for r in ksc_refs], axis=0)
    update(lat, ks, None)

    @pl.when(c == n_chunks - 1)
    def _():
        tail_lat[0:dec_seq, :] = new_rows_ref[...]
        tail_ks[0:dec_seq, :] = new_ks_ref[...]
        row = lax.broadcasted_iota(jnp.int32, (rows, page), 0)
        col = lax.broadcasted_iota(jnp.int32, (rows, page), 1)
        valid = col <= (row % dec_seq)
        update(tail_lat[...], tail_ks[...], valid)
        o_ref[...] = acc_scr[...] / l_scr[:, 0:1]


def _paged_attention(layer, page_table, q_s, e_sel, cache_latent, cache_kscale, new_rows, new_ks, pages_per_step):
    bs, rows, _ = q_s.shape
    n_pages = page_table.shape[1]
    page = cache_latent.shape[2]
    dec_seq = new_rows.shape[1]
    g = pages_per_step
    assert n_pages % g == 0
    nc = n_pages // g
    lat_w = KV_LORA + ROPE_DIM

    def lat_map(k):
        return lambda b, c, pt: (layer, pt[b, c * g + k], 0, 0)

    in_specs = [pl.BlockSpec((None, rows, lat_w), lambda b, c, pt: (b, 0, 0)),
                pl.BlockSpec((rows, MLA_HEADS), lambda b, c, pt: (0, 0))]
    in_specs += [pl.BlockSpec((None, None, page, lat_w), lat_map(k)) for k in range(g)]
    in_specs += [pl.BlockSpec((None, None, page, MLA_HEADS), lat_map(k)) for k in range(g)]
    in_specs += [pl.BlockSpec((None, dec_seq, lat_w), lambda b, c, pt: (b, 0, 0)),
                 pl.BlockSpec((None, dec_seq, MLA_HEADS), lambda b, c, pt: (b, 0, 0))]
    grid_spec = pltpu.PrefetchScalarGridSpec(
        num_scalar_prefetch=1,
        grid=(bs, nc),
        in_specs=in_specs,
        out_specs=pl.BlockSpec((None, rows, KV_LORA), lambda b, c, pt: (b, 0, 0)),
        scratch_shapes=[pltpu.VMEM((rows, LANES), F32), pltpu.VMEM((rows, LANES), F32),
                        pltpu.VMEM((rows, KV_LORA), F32),
                        pltpu.VMEM((page, lat_w), F32), pltpu.VMEM((page, MLA_HEADS), F32)])
    return pl.pallas_call(
        functools.partial(_paged_kernel, pages_per_step=g, n_chunks=nc, dec_seq=dec_seq, page=page),
        grid_spec=grid_spec,
        out_shape=jax.ShapeDtypeStruct((bs, rows, KV_LORA), F32),
        compiler_params=_params("arbitrary", "arbitrary"),
        name="paged_attention",
    )(page_table, q_s, e_sel, *([cache_latent] * g), *([cache_kscale] * g), new_rows, new_ks)


def _headmm_kernel(x_ref, w_ref, o_ref):
    o_ref[...] = jnp.dot(x_ref[...], w_ref[...], preferred_element_type=F32).astype(o_ref.dtype)


def _sample_vproj(o_lat_h, w_uv_h):
    h, ms, k = o_lat_h.shape
    return pl.pallas_call(
        _headmm_kernel,
        grid=(h,),
        in_specs=[pl.BlockSpec((None, ms, k), lambda i: (i, 0, 0)),
                  pl.BlockSpec((None, k, V_DIM), lambda i: (i, 0, 0))],
        out_specs=pl.BlockSpec((ms, V_DIM), lambda i: (0, i)),
        out_shape=jax.ShapeDtypeStruct((ms, h * V_DIM), BF16),
        compiler_params=_params("parallel"),
        name="sample_vproj",
    )(o_lat_h, w_uv_h)


def _ln_silu(y, g, b):
    yc = y - jnp.mean(y, axis=-1, keepdims=True)
    var = jnp.mean(yc * yc, axis=-1, keepdims=True)
    return _silu(yc * lax.rsqrt(var + EPS) * g + b)


CONV_HALO = 32
CONV_ROWS = 32


def _conv_prompt_kernel(a_ref, g_ref, w_ref, bdw_ref, gln_ref, bln_ref, yc_ref, st_ref, buf, *, t_rows, nt):
    t = pl.program_id(1)

    @pl.when(t == 0)
    def _():
        buf[0:CONV_HALO, :] = jnp.zeros((CONV_HALO, CONV_CH), F32)

    @pl.when(t > 0)
    def _():
        buf[0:CONV_HALO, :] = buf[t_rows:t_rows + CONV_HALO, :]

    buf[CONV_HALO:CONV_HALO + t_rows, :] = a_ref[...] * _sigmoid(g_ref[...])
    off = CONV_HALO - (CONV_K - 1)
    for r in range(t_rows // CONV_ROWS):
        base = r * CONV_ROWS
        acc = jnp.broadcast_to(bdw_ref[...], (CONV_ROWS, CONV_CH))
        for j in range(CONV_K):
            acc = acc + buf[base + off + j: base + off + j + CONV_ROWS, :] * w_ref[j:j + 1, :]
        yc_ref[base:base + CONV_ROWS, :] = _ln_silu(acc, gln_ref[...], bln_ref[...]).astype(BF16)

    @pl.when(t == nt - 1)
    def _():
        st_ref[...] = buf[t_rows + off:t_rows + CONV_HALO, :]


def _conv_prompt(z, w_dw, b_dw, g_ln, b_ln, batch, seq):
    t_rows = _pick(seq, 256, CONV_ROWS)
    nt = seq // t_rows
    mp = batch * seq
    row = lambda b, t: (0, 0)
    return pl.pallas_call(
        functools.partial(_conv_prompt_kernel, t_rows=t_rows, nt=nt),
        grid=(batch, nt),
        in_specs=[pl.BlockSpec((t_rows, CONV_CH), lambda b, t: (b * nt + t, Z_CONV_A // CONV_CH)),
                  pl.BlockSpec((t_rows, CONV_CH), lambda b, t: (b * nt + t, Z_CONV_G // CONV_CH)),
                  pl.BlockSpec((CONV_HALO, CONV_CH), row),
                  pl.BlockSpec((1, CONV_CH), row), pl.BlockSpec((1, CONV_CH), row), pl.BlockSpec((1, CONV_CH), row)],
        out_specs=[pl.BlockSpec((t_rows, CONV_CH), lambda b, t: (b * nt + t, 0)),
                   pl.BlockSpec((None, CONV_K - 1, CONV_CH), lambda b, t: (b, 0, 0))],
        out_shape=[jax.ShapeDtypeStruct((mp, CONV_CH), BF16),
                   jax.ShapeDtypeStruct((batch, CONV_K - 1, CONV_CH), F32)],
        scratch_shapes=[pltpu.VMEM((t_rows + CONV_HALO, CONV_CH), F32)],
        compiler_params=_params("parallel", "arbitrary"),
        name="conv_prompt",
    )(z, z, w_dw, b_dw, g_ln, b_ln)


def _conv_sample_kernel(a_ref, g_ref, st_ref, w_ref, bdw_ref, gln_ref, bln_ref, yc_ref, nst_ref, buf, ybuf,
                        *, n_seq, dec_seq):
    hist = CONV_K - 1
    a = a_ref[...] * _sigmoid(g_ref[...])
    w = w_ref[0:CONV_K, :]
    for s in range(n_seq):
        buf[s, 0:hist, :] = st_ref[s]
        buf[s, hist:hist + dec_seq, :] = a[s * dec_seq:(s + 1) * dec_seq, :]
    for s in range(n_seq):
        for t in range(dec_seq):
            y = jnp.sum(buf[s, t:t + CONV_K, :] * w, axis=0, keepdims=True) + bdw_ref[...]
            ybuf[s * dec_seq + t: s * dec_seq + t + 1, :] = y
        nst_ref[s] = buf[s, dec_seq:dec_seq + hist, :]
    yc_ref[...] = _ln_silu(ybuf[...], gln_ref[...], bln_ref[...]).astype(BF16)


def _conv_sample(layer, z, state_conv, w_dw, b_dw, g_ln, b_ln, mp, bs, dec_seq):
    n_seq = _pick(bs, 8, 1)
    rows = n_seq * dec_seq
    assert rows % 16 == 0 and mp % rows == 0
    hist = CONV_K - 1
    row = lambda i: (0, 0)
    return pl.pallas_call(
        functools.partial(_conv_sample_kernel, n_seq=n_seq, dec_seq=dec_seq),
        grid=(bs // n_seq,),
        in_specs=[pl.BlockSpec((rows, CONV_CH), lambda i: (mp // rows + i, Z_CONV_A // CONV_CH)),
                  pl.BlockSpec((rows, CONV_CH), lambda i: (mp // rows + i, Z_CONV_G // CONV_CH)),
                  pl.BlockSpec((None, n_seq, hist, CONV_CH), lambda i: (layer, i, 0, 0)),
                  pl.BlockSpec((CONV_HALO, CONV_CH), row),
                  pl.BlockSpec((1, CONV_CH), row), pl.BlockSpec((1, CONV_CH), row), pl.BlockSpec((1, CONV_CH), row)],
        out_specs=[pl.BlockSpec((rows, CONV_CH), lambda i: (i, 0)),
                   pl.BlockSpec((n_seq, hist, CONV_CH), lambda i: (i, 0, 0))],
        out_shape=[jax.ShapeDtypeStruct((bs * dec_seq, CONV_CH), BF16),
                   jax.ShapeDtypeStruct((bs, hist, CONV_CH), F32)],
        scratch_shapes=[pltpu.VMEM((n_seq, hist + SUBLANES + 2, CONV_CH), F32), pltpu.VMEM((rows, CONV_CH), F32)],
        compiler_params=_params("parallel"),
        name="conv_sample",
    )(z, z, state_conv, w_dw, b_dw, g_ln, b_ln)


def _hgrn_gates(hq, hf, hi, lb):
    log_lb = jnp.log(jnp.maximum(lb, LB_FLOOR))
    log_1m = jnp.log1p(-lb)
    log_sig = jnp.minimum(hf, 0.0) - jnp.log1p(jnp.exp(-jnp.abs(hf)))
    bb = log_1m + log_sig
    logf = jnp.maximum(log_lb, bb) + jnp.log1p(jnp.exp(-jnp.abs(log_lb - bb)))
    k = 1.0 - jnp.exp(logf)
    return _silu(hq), k, _silu(hi), logf


def _chunk_cumsum(bd, logf):
    out = None
    for part in _split3(logf):
        t = jnp.dot(bd, part, preferred_element_type=F32)
        out = t if out is None else out + t
    return out


def _intra_weights(q, k, cum, valid_fn, n_rows):
    lane = lax.broadcasted_iota(jnp.int32, (SUBLANES, LANES), 1)
    srow = lax.broadcasted_iota(jnp.int32, (SUBLANES, LANES), 0)
    blocks = [jnp.zeros((SUBLANES, LANES), F32) for _ in range(n_rows // SUBLANES)]
    for t in range(n_rows):
        qt = q[t:t + 1, :]
        ct = cum[t:t + 1, :]
        for rb in range(n_rows // SUBLANES):
            mask = valid_fn(srow + rb * SUBLANES, t, rb)
            if mask is None:
                continue
            sl = slice(rb * SUBLANES, (rb + 1) * SUBLANES)
            d = ct - cum[sl, :]
            if mask is not True:
                d = jnp.where(mask, d, NEG_BIG)
            w = jnp.sum(jnp.exp(d) * (qt * k[sl, :]), axis=-1, keepdims=True)
            blocks[rb] = jnp.where(lane == t, w, blocks[rb])
    return jnp.concatenate(blocks, axis=0)


def _causal_valid(s_idx, t, rb):
    if rb * SUBLANES > t:
        return None
    if (rb + 1) * SUBLANES - 1 <= t:
        return True
    return s_idx <= t


def _hgrn_prompt_kernel(hq_ref, hf_ref, hi_ref, hg_ref, lb_ref, go_ref, bd_ref, o_ref, s_ref,
                        st_scr, q_scr, k_scr, v_scr, c_scr, o_scr, *, t_rows, nt, chunk):
    t = pl.program_id(2)

    @pl.when(t == 0)
    def _():
        st_scr[...] = jnp.zeros(st_scr.shape, F32)

    q, k, v, logf = _hgrn_gates(hq_ref[...], hf_ref[...], hi_ref[...], lb_ref[...])
    q_scr[...] = q
    k_scr[...] = k
    v_scr[...] = v
    c_scr[...] = _chunk_cumsum(bd_ref[...], logf)

    def body(ci, carry):
        r0 = pl.multiple_of(ci * chunk, chunk)
        qc = q_scr[pl.ds(r0, chunk), :]
        kc = k_scr[pl.ds(r0, chunk), :]
        vc = v_scr[pl.ds(r0, chunk), :]
        cum = c_scr[pl.ds(r0, chunk), :]
        st = st_scr[...]
        vb = vc.astype(BF16)
        inter = lax.dot_general((qc * jnp.exp(cum)).astype(BF16), st.astype(BF16),
                                (((1,), (1,)), ((), ())), preferred_element_type=F32)
        att_t = _intra_weights(qc, kc, cum, _causal_valid, chunk)
        intra = lax.dot_general(att_t.astype(BF16), vb, (((0,), (0,)), ((), ())),
                                preferred_element_type=F32)
        o_scr[pl.ds(r0, chunk), :] = inter + intra[0:chunk, :]
        last = cum[chunk - 1:chunk, :]
        kt = (kc * jnp.exp(last - cum)).astype(BF16)
        st_scr[...] = st * jnp.exp(last) + lax.dot_general(vb, kt, (((0,), (0,)), ((), ())),
                                                           preferred_element_type=F32)
        return carry

    lax.fori_loop(0, t_rows // chunk, body, 0)
    o = o_scr[...]
    o_ref[...] = (_rms_rows(o, go_ref[...]) * _sigmoid(hg_ref[...])).astype(BF16)

    @pl.when(t == nt - 1)
    def _():
        s_ref[...] = st_scr[...].T


def _block_tri(n, block):
    r = jnp.arange(n)
    return ((r[:, None] >= r[None, :]) & ((r[:, None] // block) == (r[None, :] // block))).astype(BF16)


def _hgrn_prompt(z, lb, g_o, batch, seq):
    chunk = HGRN_CHUNK if seq % HGRN_CHUNK == 0 else seq
    t_rows = _pick(seq, 256, chunk)
    nt = seq // t_rows
    mp = batch * seq
    bd = _block_tri(t_rows, chunk)

    def zmap(off):
        return lambda b, h, t: (b * nt + t, off // HGRN_DK + h)

    return pl.pallas_call(
        functools.partial(_hgrn_prompt_kernel, t_rows=t_rows, nt=nt, chunk=chunk),
        grid=(batch, HGRN_HEADS, nt),
        in_specs=[pl.BlockSpec((t_rows, HGRN_DK), zmap(Z_HQ)),
                  pl.BlockSpec((t_rows, HGRN_DK), zmap(Z_HF)),
                  pl.BlockSpec((t_rows, HGRN_DK), zmap(Z_HI)),
                  pl.BlockSpec((t_rows, HGRN_DK), zmap(Z_HG)),
                  pl.BlockSpec((1, HGRN_DK), lambda b, h, t: (0, h)),
                  pl.BlockSpec((1, HGRN_DV), lambda b, h, t: (0, 0)),
                  pl.BlockSpec((t_rows, t_rows), lambda b, h, t: (0, 0))],
        out_specs=[pl.BlockSpec((t_rows, HGRN_DV), lambda b, h, t: (b * nt + t, h)),
                   pl.BlockSpec((None, None, HGRN_DK, HGRN_DV), lambda b, h, t: (b, h, 0, 0))],
        out_shape=[jax.ShapeDtypeStruct((mp, HGRN_HEADS * HGRN_DV), BF16),
                   jax.ShapeDtypeStruct((batch, HGRN_HEADS, HGRN_DK, HGRN_DV), F32)],
        scratch_shapes=[pltpu.VMEM((HGRN_DV, HGRN_DK), F32)] + [pltpu.VMEM((t_rows, HGRN_DK), F32)] * 5,
        compiler_params=_params("parallel", "parallel", "arbitrary"),
        name="hgrn_prompt",
    )(z, z, z, z, lb, g_o, bd)


def _hgrn_sample_kernel(hq_ref, hf_ref, hi_ref, hg_ref, lb_ref, go_ref, bd_ref, s0_ref, o_ref, s_ref,
                        *, n_seq, dec_seq):
    rows = n_seq * dec_seq
    q, k, v, logf = _hgrn_gates(hq_ref[...], hf_ref[...], hi_ref[...], lb_ref[...])
    cum = _chunk_cumsum(bd_ref[...], logf)
    vb = v.astype(BF16)
    qe = (q * jnp.exp(cum)).astype(BF16)

    def valid(s_idx, t, rb):
        lo = (t // dec_seq) * dec_seq
        if lo // SUBLANES != rb:
            return None
        return jnp.logical_and(s_idx >= lo, s_idx <= t)

    att_t = _intra_weights(q, k, cum, valid, rows)
    intra = lax.dot_general(att_t.astype(BF16), vb, (((0,), (0,)), ((), ())),
                            preferred_element_type=F32)
    row = lax.broadcasted_iota(jnp.int32, (rows, HGRN_DK), 0)
    o = intra[0:rows, :]
    for s in range(n_seq):
        in_seq = (row // dec_seq) == s
        st = s0_ref[s].T
        inter = lax.dot_general(qe, st.astype(BF16), (((1,), (1,)), ((), ())), preferred_element_type=F32)
        o = o + jnp.where(in_seq, inter, 0.0)
        last = cum[(s + 1) * dec_seq - 1:(s + 1) * dec_seq, :]
        kt = (k * jnp.exp(jnp.where(in_seq, last - cum, NEG_BIG))).astype(BF16)
        st_new = st * jnp.exp(last) + lax.dot_general(vb, kt, (((0,), (0,)), ((), ())),
                                                      preferred_element_type=F32)
        s_ref[s] = st_new.T
    o_ref[...] = (_rms_rows(o, go_ref[...]) * _sigmoid(hg_ref[...])).astype(BF16)


def _hgrn_sample(layer, z, state_hgrn, lb, g_o, mp, bs, dec_seq):
    n_seq = _pick(bs, max(1, 16 // dec_seq), 1)
    rows = n_seq * dec_seq
    assert rows % 16 == 0 and mp % rows == 0 and SUBLANES % dec_seq == 0
    bd = _block_tri(rows, dec_seq)

    def zmap(off):
        return lambda i, h: (mp // rows + i, off // HGRN_DK + h)

    return pl.pallas_call(
        functools.partial(_hgrn_sample_kernel, n_seq=n_seq, dec_seq=dec_seq),
        grid=(bs // n_seq, HGRN_HEADS),
        in_specs=[pl.BlockSpec((rows, HGRN_DK), zmap(Z_HQ)),
                  pl.BlockSpec((rows, HGRN_DK), zmap(Z_HF)),
                  pl.BlockSpec((rows, HGRN_DK), zmap(Z_HI)),
                  pl.BlockSpec((rows, HGRN_DK), zmap(Z_HG)),
                  pl.BlockSpec((1, HGRN_DK), lambda i, h: (0, h)),
                  pl.BlockSpec((1, HGRN_DV), lambda i, h: (0, 0)),
                  pl.BlockSpec((rows, rows), lambda i, h: (0, 0)),
                  pl.BlockSpec((None, n_seq, None, HGRN_DK, HGRN_DV), lambda i, h: (layer, i, h, 0, 0))],
        out_specs=[pl.BlockSpec((rows, HGRN_DV), lambda i, h: (i, h)),
                   pl.BlockSpec((n_seq, None, HGRN_DK, HGRN_DV), lambda i, h: (i, h, 0, 0))],
        out_shape=[jax.ShapeDtypeStruct((bs * dec_seq, HGRN_HEADS * HGRN_DV), BF16),
                   jax.ShapeDtypeStruct((bs, HGRN_HEADS, HGRN_DK, HGRN_DV), F32)],
        compiler_params=_params("parallel", "parallel"),
        name="hgrn_sample",
    )(z, z, z, z, lb, g_o, bd, state_hgrn)


def _mix_kernel(om_ref, yc_ref, oh_ref, wm_ref, wc_ref, wh_ref, g0_ref, g1_ref, g2_ref, mix_ref):
    ym = jnp.dot(om_ref[...], wm_ref[...], preferred_element_type=F32)
    yc = jnp.dot(yc_ref[...], wc_ref[...], preferred_element_type=F32)
    yh = jnp.dot(oh_ref[...], wh_ref[...], preferred_element_type=F32)
    mix = _sigmoid(g0_ref[...]) * ym + _sigmoid(g1_ref[...]) * yc + _sigmoid(g2_ref[...]) * yh
    mix_ref[...] = mix.astype(BF16)


def _mix(o_mla, yc, o_h, w_o_mla, w_pw, w_o_hgrn, z):
    m = o_mla.shape[0]
    bm = _pick(m, 768, 16)
    bn = 512

    def gmap(off):
        return lambda i, j: (i, off // bn + j)

    return pl.pallas_call(
        _mix_kernel,
        grid=(m // bm, D_MODEL // bn),
        in_specs=[pl.BlockSpec((bm, MLA_HEADS * V_DIM), lambda i, j: (i, 0)),
                  pl.BlockSpec((bm, CONV_CH), lambda i, j: (i, 0)),
                  pl.BlockSpec((bm, HGRN_HEADS * HGRN_DV), lambda i, j: (i, 0)),
                  pl.BlockSpec((MLA_HEADS * V_DIM, bn), lambda i, j: (0, j)),
                  pl.BlockSpec((CONV_CH, bn), lambda i, j: (0, j)),
                  pl.BlockSpec((HGRN_HEADS * HGRN_DV, bn), lambda i, j: (0, j)),
                  pl.BlockSpec((bm, bn), gmap(Z_G0)),
                  pl.BlockSpec((bm, bn), gmap(Z_G1)),
                  pl.BlockSpec((bm, bn), gmap(Z_G2))],
        out_specs=pl.BlockSpec((bm, bn), lambda i, j: (i, j)),
        out_shape=jax.ShapeDtypeStruct((m, D_MODEL), BF16),
        compiler_params=_params("parallel", "arbitrary"),
        name="branch_mix",
    )(o_mla, yc, o_h, w_o_mla, w_pw, w_o_hgrn, z, z, z)


def _resmm_kernel(a_ref, w_ref, x_ref, o_ref):
    o_ref[...] = x_ref[...] + jnp.dot(a_ref[...], w_ref[...], preferred_element_type=F32)


def _residual_matmul(a, w, x, bm_target, bn):
    m, k = a.shape
    n = w.shape[1]
    bm = _pick(m, bm_target, 16)
    return pl.pallas_call(
        _resmm_kernel,
        grid=(m // bm, n // bn),
        in_specs=[pl.BlockSpec((bm, k), lambda i, j: (i, 0)),
                  pl.BlockSpec((k, bn), lambda i, j: (0, j)),
                  pl.BlockSpec((bm, bn), lambda i, j: (i, j))],
        out_specs=pl.BlockSpec((bm, bn), lambda i, j: (i, j)),
        out_shape=jax.ShapeDtypeStruct((m, n), F32),
        compiler_params=_params("parallel", "arbitrary"),
        name="residual_matmul",
    )(a, w, x)


def _ffn_up_kernel(x_ref, g_ref, wg_ref, wu_ref, o_ref, h_scr):
    @pl.when(pl.program_id(1) == 0)
    def _():
        h_scr[...] = _rms_rows(x_ref[...], g_ref[...]).astype(BF16)

    h = h_scr[...]
    gate = jnp.dot(h, wg_ref[...], preferred_element_type=F32)
    up = jnp.dot(h, wu_ref[...], preferred_element_type=F32)
    o_ref[...] = (_silu(gate) * up).astype(BF16)


def _ffn_up(x, g, w_gu):
    m, k = x.shape
    bm = _pick(m, 1536, 16)
    bn = 512
    nj = D_FF // bn
    return pl.pallas_call(
        _ffn_up_kernel,
        grid=(m // bm, nj),
        in_specs=[pl.BlockSpec((bm, k), lambda i, j: (i, 0)),
                  pl.BlockSpec((1, k), lambda i, j: (0, 0)),
                  pl.BlockSpec((k, bn), lambda i, j: (0, j)),
                  pl.BlockSpec((k, bn), lambda i, j: (0, nj + j))],
        out_specs=pl.BlockSpec((bm, bn), lambda i, j: (i, j)),
        out_shape=jax.ShapeDtypeStruct((m, D_FF), BF16),
        scratch_shapes=[pltpu.VMEM((bm, k), BF16)],
        compiler_params=_params("parallel", "arbitrary"),
        name="ffn_up",
    )(x, g, w_gu, w_gu)


def _ple_kernel(x_ref, g_ref, wg_ref, p_ref, wp_ref, xr_ref, o_ref, h_scr, p_scr):
    @pl.when(pl.program_id(1) == 0)
    def _():
        h_scr[...] = _rms_rows(x_ref[...], g_ref[...]).astype(BF16)
        p_scr[...] = p_ref[...].astype(BF16)

    gate = _sigmoid(jnp.dot(h_scr[...], wg_ref[...], preferred_element_type=F32))
    proj = jnp.dot(p_scr[...], wp_ref[...], preferred_element_type=F32)
    o_ref[...] = xr_ref[...] + proj * gate


def _ple(x, g, w_gate, p, w_proj):
    m, k = x.shape
    bm = _pick(m, 768, 16)
    bn = 512
    return pl.pallas_call(
        _ple_kernel,
        grid=(m // bm, D_MODEL // bn),
        in_specs=[pl.BlockSpec((bm, k), lambda i, j: (i, 0)),
                  pl.BlockSpec((1, k), lambda i, j: (0, 0)),
                  pl.BlockSpec((k, bn), lambda i, j: (0, j)),
                  pl.BlockSpec((bm, PLE_DIM), lambda i, j: (i, 0)),
                  pl.BlockSpec((PLE_DIM, bn), lambda i, j: (0, j)),
                  pl.BlockSpec((bm, bn), lambda i, j: (i, j))],
        out_specs=pl.BlockSpec((bm, bn), lambda i, j: (i, j)),
        out_shape=jax.ShapeDtypeStruct((m, D_MODEL), F32),
        scratch_shapes=[pltpu.VMEM((bm, k), BF16), pltpu.VMEM((bm, PLE_DIM), BF16)],
        compiler_params=_params("parallel", "arbitrary"),
        name="ple",
    )(x, g, w_gate, p, w_proj, x)


def _rope_tables(pos):
    half = ROPE_DIM // 2
    inv = ROPE_THETA ** (-jnp.arange(half, dtype=F32) / half)
    ang = pos.astype(F32)[:, None] * inv[None, :]
    cos, sin = jnp.cos(ang), jnp.sin(ang)
    reps = LANES // ROPE_DIM
    cos_t = jnp.tile(jnp.concatenate([cos, cos], axis=-1), (1, reps))
    sin_t = jnp.tile(jnp.concatenate([-sin, sin], axis=-1), (1, reps))
    return cos_t, sin_t


def _row(v):
    return v.reshape(1, -1).astype(F32)


def kernel(x_prompt, x_sample, cache_latent, cache_kscale, state_conv, state_hgrn, page_table, p_prompt, p_sample, g_mix, w_in, g_q, w_uq, g_kv, g_qk, g_kk, w_uk, w_uv, w_o_mla, w_dw, b_dw, g_cln, b_cln, w_pw, lb_logits, g_hgrn, w_o_hgrn, w_out, g_ffn, w_gu, w_down, g_ple, w_ple_gate, w_ple_proj):
    depth = w_in.shape[0]
    batch, seq, _ = x_prompt.shape
    bs, dec_seq, _ = x_sample.shape
    mp, ms = batch * seq, bs * dec_seq
    n_pages, page = page_table.shape[1], cache_latent.shape[2]
    past = n_pages * page

    lb_p = jax.nn.softmax(lb_logits.astype(F32), axis=0)
    lb_all = jnp.cumsum(lb_p, axis=0) - lb_p[0:1]

    cos_p, sin_p = _rope_tables(jnp.arange(seq, dtype=jnp.int32))
    cos_s, sin_s = _rope_tables(past + jnp.arange(dec_seq, dtype=jnp.int32))
    cos_t = jnp.concatenate([jnp.tile(cos_p, (batch, 1)), jnp.tile(cos_s, (bs, 1))], axis=0)
    sin_t = jnp.concatenate([jnp.tile(sin_p, (batch, 1)), jnp.tile(sin_s, (bs, 1))], axis=0)

    rows_q = MLA_HEADS * dec_seq
    e_sel = (jnp.arange(rows_q)[:, None] // dec_seq == jnp.arange(MLA_HEADS)[None, :]).astype(BF16)
    pages_per_step = _pick(n_pages, 16, 1)

    x = jnp.concatenate([x_prompt.reshape(mp, D_MODEL), x_sample.reshape(ms, D_MODEL)], axis=0)
    lat_p, ks_p, conv_p, hg_p, lat_s, ks_s, conv_s, hg_s = ([] for _ in range(8))
    for i in range(depth):
        wi = w_in[i]
        w_in_r = jnp.concatenate(
            [wi[:, 0:Q_LORA + KV_LORA], wi[:, Q_LORA + KV_LORA + ROPE_DIM:], wi[:, Q_LORA + KV_LORA:Q_LORA + KV_LORA + ROPE_DIM],
             jnp.zeros((D_MODEL, Z_COLS - wi.shape[1]), wi.dtype)], axis=1).astype(BF16)
        wq = w_uq[i].reshape(Q_LORA, MLA_HEADS, QK_DIM)
        w_uq_r = jnp.concatenate([wq[:, :, 0:NOPE_DIM].reshape(Q_LORA, -1), wq[:, :, NOPE_DIM:].reshape(Q_LORA, -1)],
                                 axis=1).astype(BF16)
        w_uk2 = w_uk[i].reshape(KV_LORA, -1).astype(BF16)
        w_uv2 = w_uv[i].reshape(KV_LORA, -1).astype(BF16)
        w_uk_t = jnp.transpose(w_uk[i], (1, 2, 0)).astype(BF16)
        w_uv_h = jnp.transpose(w_uv[i], (1, 0, 2)).astype(BF16)
        gqkn = _row(g_qk[i, 0:NOPE_DIM])
        gkkn = _row(g_kk[i, 0:NOPE_DIM])
        gqkr = _row(jnp.tile(g_qk[i, NOPE_DIM:], LANES // ROPE_DIM))
        gkkr = _row(jnp.concatenate([g_kk[i, NOPE_DIM:], jnp.zeros((LANES - ROPE_DIM,), F32)]))
        w_dw_p = jnp.concatenate([w_dw[i], jnp.zeros((CONV_HALO - CONV_K, CONV_CH), F32)], axis=0)
        lb = _row(lb_all[i])

        z = _inproj(x, _row(g_mix[i]), w_in_r)
        q_all = _qproj(z, _row(g_q[i]), w_uq_r, gqkn, gkkn, gqkr, cos_t, sin_t)
        rows, kscale, k_all, v_all = _kvproj(z, _row(g_kv[i]), gkkr, w_uk2, w_uv2, cos_t, sin_t)

        o_mla_p = _flash(q_all, k_all, v_all, batch, seq)
        q_s = _absorb(q_all, w_uk_t, mp, ms)
        q_s = q_s.reshape(MLA_HEADS, bs, dec_seq, -1).transpose(1, 0, 2, 3).reshape(bs, rows_q, -1)
        new_rows = rows[mp:].reshape(bs, dec_seq, -1)
        new_ks = kscale[mp:].reshape(bs, dec_seq, MLA_HEADS)
        o_lat = _paged_attention(i, page_table, q_s, e_sel, cache_latent, cache_kscale, new_rows, new_ks,
                                 pages_per_step)
        o_lat_h = o_lat.reshape(bs, MLA_HEADS, dec_seq, KV_LORA).transpose(1, 0, 2, 3)
        o_lat_h = o_lat_h.reshape(MLA_HEADS, ms, KV_LORA).astype(BF16)
        o_mla_s = _sample_vproj(o_lat_h, w_uv_h)
        o_mla = jnp.concatenate([o_mla_p, o_mla_s], axis=0)

        yc_p, conv_new_p = _conv_prompt(z, w_dw_p, _row(b_dw[i]), _row(g_cln[i]), _row(b_cln[i]), batch, seq)
        yc_s, conv_new_s = _conv_sample(i, z, state_conv, w_dw_p, _row(b_dw[i]), _row(g_cln[i]), _row(b_cln[i]),
                                        mp, bs, dec_seq)
        yc = jnp.concatenate([yc_p, yc_s], axis=0)

        oh_p, s_new_p = _hgrn_prompt(z, lb, _row(g_hgrn[i]), batch, seq)
        oh_s, s_new_s = _hgrn_sample(i, z, state_hgrn, lb, _row(g_hgrn[i]), mp, bs, dec_seq)
        o_h = jnp.concatenate([oh_p, oh_s], axis=0)

        mix = _mix(o_mla, yc, o_h, w_o_mla[i].astype(BF16), w_pw[i].astype(BF16), w_o_hgrn[i].astype(BF16), z)
        x = _residual_matmul(mix, w_out[i].astype(BF16), x, 1536, 512)

        act = _ffn_up(x, _row(g_ffn[i]), w_gu[i].astype(BF16))
        x = _residual_matmul(act, w_down[i].astype(BF16), x, 768, 512)
        p_emb = jnp.concatenate([p_prompt[i].reshape(mp, PLE_DIM), p_sample[i].reshape(ms, PLE_DIM)], axis=0)
        x = _ple(x, _row(g_ple[i]), w_ple_gate[i].astype(BF16), p_emb, w_ple_proj[i].astype(BF16))

        lat_p.append(rows[:mp].reshape(batch, seq, -1))
        ks_p.append(kscale[:mp].reshape(batch, seq, MLA_HEADS))
        conv_p.append(conv_new_p)
        hg_p.append(s_new_p)
        lat_s.append(new_rows)
        ks_s.append(new_ks)
        conv_s.append(conv_new_s)
        hg_s.append(s_new_s)

    return (x[:mp].reshape(batch, seq, D_MODEL), x[mp:].reshape(bs, dec_seq, D_MODEL),
            jnp.stack(lat_p), jnp.stack(ks_p), jnp.stack(conv_p), jnp.stack(hg_p),
            jnp.stack(lat_s), jnp.stack(ks_s), jnp.stack(conv_s), jnp.stack(hg_s))
```

```python
import functools

import jax
import jax.numpy as jnp
from jax import lax
from jax.experimental import pallas as pl
from jax.experimental.pallas import tpu as pltpu

F32 = jnp.float32
BF16 = jnp.bfloat16

D_MODEL = 2048
MLA_HEADS = 16
Q_LORA = 512
KV_LORA = 512
NOPE_DIM = 128
ROPE_DIM = 64
QK_DIM = NOPE_DIM + ROPE_DIM
V_DIM = 128
ROPE_THETA = 10000.0
ATTN_SCALE = QK_DIM ** -0.5
CONV_CH = 1024
CONV_K = 31
HGRN_HEADS = 8
HGRN_DK = 128
HGRN_DV = 128
HGRN_KW = HGRN_HEADS * HGRN_DK
HGRN_CHUNK = 32
LB_FLOOR = 1e-30
D_FF = ((8 * D_MODEL // 3 + 255) // 256) * 256
PLE_DIM = 256
EPS = 1e-6
NEG_BIG = -1e30

LANES = 128
SUBLANES = 8
VMEM_LIMIT_BYTES = 56 * 1024 * 1024

Z_CQ = 0
Z_CKV = 512
Z_CONV_A = 1024
Z_CONV_G = 2048
Z_HQ = 3072
Z_HF = 4096
Z_HI = 5120
Z_HG = 6144
Z_G0 = 7168
Z_G1 = 9216
Z_G2 = 11264
Z_KR = 13312
Z_COLS = 13824


def _pick(total, target, mult=SUBLANES):
    best = None
    for d in range(mult, min(total, target) + 1, mult):
        if total % d == 0:
            best = d
    assert best is not None, (total, target, mult)
    return best


def _params(*sem):
    return pltpu.CompilerParams(dimension_semantics=sem, vmem_limit_bytes=VMEM_LIMIT_BYTES)


def _rms_rows(x, g):
    ms = jnp.mean(x * x, axis=-1, keepdims=True)
    return x * lax.rsqrt(ms + EPS) * g


def _sigmoid(x):
    return 1.0 / (1.0 + jnp.exp(-x))


def _silu(x):
    return x * _sigmoid(x)


def _rope128(x, cos_t, sin_t):
    lane = lax.broadcasted_iota(jnp.int32, x.shape, 1)
    first_half = (lane & (ROPE_DIM // 2)) == 0
    partner = jnp.where(first_half, pltpu.roll(x, LANES - ROPE_DIM // 2, 1), pltpu.roll(x, ROPE_DIM // 2, 1))
    return x * cos_t + partner * sin_t


def _inproj_kernel(x_ref, g_ref, w_ref, z_ref, h_scr):
    @pl.when(pl.program_id(1) == 0)
    def _():
        h_scr[...] = _rms_rows(x_ref[...], g_ref[...]).astype(BF16)

    z_ref[...] = jnp.dot(h_scr[...], w_ref[...], preferred_element_type=F32)


def _inproj(x, g, w):
    m, k = x.shape
    n = w.shape[1]
    bm = _pick(m, 1536)
    bn = _pick(n, 768, LANES)
    return pl.pallas_call(
        _inproj_kernel,
        grid=(m // bm, n // bn),
        in_specs=[pl.BlockSpec((bm, k), lambda i, j: (i, 0)),
                  pl.BlockSpec((1, k), lambda i, j: (0, 0)),
                  pl.BlockSpec((k, bn), lambda i, j: (0, j))],
        out_specs=pl.BlockSpec((bm, bn), lambda i, j: (i, j)),
        out_shape=jax.ShapeDtypeStruct((m, n), F32),
        scratch_shapes=[pltpu.VMEM((bm, k), BF16)],
        compiler_params=_params("parallel", "arbitrary"),
        name="inproj",
    )(x, g, w)


def _qproj_kernel(cq_ref, gq_ref, w_ref, gqkn_ref, gkkn_ref, gqkr_ref, cos_ref, sin_ref, q_ref):
    h = _rms_rows(cq_ref[...], gq_ref[...]).astype(BF16)
    acc = jnp.dot(h, w_ref[...], preferred_element_type=F32)
    cos_t = cos_ref[...]
    sin_t = sin_ref[...]
    lane = lax.broadcasted_iota(jnp.int32, cos_t.shape, 1)
    low = lane < ROPE_DIM
    nope_w = MLA_HEADS * NOPE_DIM
    for p in range(MLA_HEADS // 2):
        r = acc[:, nope_w + LANES * p: nope_w + LANES * (p + 1)]
        r2 = r * r
        ssr0 = jnp.sum(jnp.where(low, r2, 0.0), axis=-1, keepdims=True)
        ssr1 = jnp.sum(jnp.where(low, 0.0, r2), axis=-1, keepdims=True)
        invs = []
        for u, ssr in ((0, ssr0), (1, ssr1)):
            hd = 2 * p + u
            nq = acc[:, NOPE_DIM * hd: NOPE_DIM * (hd + 1)]
            ss = jnp.sum(nq * nq, axis=-1, keepdims=True) + ssr
            inv = lax.rsqrt(ss / QK_DIM + EPS)
            invs.append(inv)
            qn = nq * inv * gqkn_ref[...] * gkkn_ref[...]
            q_ref[hd, :, 0:NOPE_DIM] = (qn * ATTN_SCALE).astype(BF16)
        rn = r * jnp.where(low, invs[0], invs[1]) * gqkr_ref[...]
        rr = (_rope128(rn, cos_t, sin_t) * ATTN_SCALE).astype(BF16)
        q_ref[2 * p, :, NOPE_DIM:QK_DIM] = rr[:, 0:ROPE_DIM]
        q_ref[2 * p + 1, :, NOPE_DIM:QK_DIM] = rr[:, ROPE_DIM:LANES]


def _qproj(z, g_q, w_uq_r, gqkn, gkkn, gqkr, cos_t, sin_t):
    m = z.shape[0]
    bm = _pick(m, 512, 16)
    n = w_uq_r.shape[1]
    return pl.pallas_call(
        _qproj_kernel,
        grid=(m // bm,),
        in_specs=[pl.BlockSpec((bm, Q_LORA), lambda i: (i, Z_CQ // Q_LORA)),
                  pl.BlockSpec((1, Q_LORA), lambda i: (0, 0)),
                  pl.BlockSpec((Q_LORA, n), lambda i: (0, 0)),
                  pl.BlockSpec((1, NOPE_DIM), lambda i: (0, 0)),
                  pl.BlockSpec((1, NOPE_DIM), lambda i: (0, 0)),
                  pl.BlockSpec((1, LANES), lambda i: (0, 0)),
                  pl.BlockSpec((bm, LANES), lambda i: (i, 0)),
                  pl.BlockSpec((bm, LANES), lambda i: (i, 0))],
        out_specs=pl.BlockSpec((MLA_HEADS, bm, QK_DIM), lambda i: (0, i, 0)),
        out_shape=jax.ShapeDtypeStruct((MLA_HEADS, m, QK_DIM), BF16),
        compiler_params=_params("parallel"),
        name="qproj",
    )(z, g_q, w_uq_r, gqkn, gkkn, gqkr, cos_t, sin_t)


def _kvproj_kernel(ckv_ref, kr_ref, gkv_ref, gkkr_ref, wuk_ref, wuv_ref, cos_ref, sin_ref,
                   rows_ref, ks_ref, k_ref, v_ref):
    c = _rms_rows(ckv_ref[...], gkv_ref[...])
    rows_ref[:, 0:KV_LORA] = c
    cb = c.astype(BF16)
    kn = jnp.dot(cb, wuk_ref[...], preferred_element_type=F32)
    v = jnp.dot(cb, wuv_ref[...], preferred_element_type=F32)
    kr = kr_ref[...]
    ss_kr = jnp.sum(kr * kr, axis=-1, keepdims=True)
    kpe = _rope128(kr * gkkr_ref[...], cos_ref[...], sin_ref[...])
    rows_ref[:, KV_LORA:KV_LORA + ROPE_DIM] = kpe[:, 0:ROPE_DIM]
    lane = lax.broadcasted_iota(jnp.int32, kr.shape, 1)
    ks_all = jnp.zeros(kr.shape, F32)
    for hd in range(MLA_HEADS):
        knh = kn[:, NOPE_DIM * hd: NOPE_DIM * (hd + 1)]
        ks = lax.rsqrt((jnp.sum(knh * knh, axis=-1, keepdims=True) + ss_kr) / QK_DIM + EPS)
        k_ref[hd, :, 0:NOPE_DIM] = (knh * ks).astype(BF16)
        k_ref[hd, :, NOPE_DIM:QK_DIM] = (kpe * ks).astype(BF16)[:, 0:ROPE_DIM]
        v_ref[hd] = v[:, V_DIM * hd: V_DIM * (hd + 1)].astype(BF16)
        ks_all = jnp.where(lane == hd, ks, ks_all)
    ks_ref[...] = ks_all[:, 0:MLA_HEADS]


def _kvproj(z, g_kv, gkkr, w_uk2, w_uv2, cos_t, sin_t):
    m = z.shape[0]
    bm = _pick(m, 512, 16)
    return pl.pallas_call(
        _kvproj_kernel,
        grid=(m // bm,),
        in_specs=[pl.BlockSpec((bm, KV_LORA), lambda i: (i, Z_CKV // KV_LORA)),
                  pl.BlockSpec((bm, LANES), lambda i: (i, Z_KR // LANES)),
                  pl.BlockSpec((1, KV_LORA), lambda i: (0, 0)),
                  pl.BlockSpec((1, LANES), lambda i: (0, 0)),
                  pl.BlockSpec((KV_LORA, MLA_HEADS * NOPE_DIM), lambda i: (0, 0)),
                  pl.BlockSpec((KV_LORA, MLA_HEADS * V_DIM), lambda i: (0, 0)),
                  pl.BlockSpec((bm, LANES), lambda i: (i, 0)),
                  pl.BlockSpec((bm, LANES), lambda i: (i, 0))],
        out_specs=[pl.BlockSpec((bm, KV_LORA + ROPE_DIM), lambda i: (i, 0)),
                   pl.BlockSpec((bm, MLA_HEADS), lambda i: (i, 0)),
                   pl.BlockSpec((MLA_HEADS, bm, QK_DIM), lambda i: (0, i, 0)),
                   pl.BlockSpec((MLA_HEADS, bm, V_DIM), lambda i: (0, i, 0))],
        out_shape=[jax.ShapeDtypeStruct((m, KV_LORA + ROPE_DIM), F32),
                   jax.ShapeDtypeStruct((m, MLA_HEADS), F32),
                   jax.ShapeDtypeStruct((MLA_HEADS, m, QK_DIM), BF16),
                   jax.ShapeDtypeStruct((MLA_HEADS, m, V_DIM), BF16)],
        compiler_params=_params("parallel"),
        name="kvproj",
    )(z, z, g_kv, gkkr, w_uk2, w_uv2, cos_t, sin_t)


def _flash_kernel(q_ref, k_ref, v_ref, o_ref, *, tq):
    qi = pl.program_id(2)
    q = q_ref[...]

    def step(kb, carry, masked):
        m, l, acc = carry
        r0 = pl.multiple_of(kb * tq, tq)
        k = k_ref[pl.ds(r0, tq), :]
        v = v_ref[pl.ds(r0, tq), :]
        s = lax.dot_general(q, k, (((1,), (1,)), ((), ())), preferred_element_type=F32)
        if masked:
            row = lax.broadcasted_iota(jnp.int32, s.shape, 0)
            col = lax.broadcasted_iota(jnp.int32, s.shape, 1)
            s = jnp.where(col <= row, s, NEG_BIG)
        m_new = jnp.maximum(m, jnp.max(s, axis=-1, keepdims=True))
        p = jnp.exp(s - m_new)
        corr = jnp.exp(m - m_new)
        l = l * corr + jnp.sum(p, axis=-1, keepdims=True)
        acc = acc * corr + jnp.dot(p.astype(BF16), v, preferred_element_type=F32)
        return m_new, l, acc

    init = (jnp.full((tq, 1), NEG_BIG, F32), jnp.zeros((tq, 1), F32), jnp.zeros((tq, V_DIM), F32))
    carry = lax.fori_loop(0, qi, lambda kb, c: step(kb, c, False), init)
    m, l, acc = step(qi, carry, True)
    o_ref[...] = (acc / l).astype(o_ref.dtype)


def _flash(q, k, v, batch, seq):
    tq = _pick(seq, 512, 16)
    nq = seq // tq
    mp = batch * seq
    return pl.pallas_call(
        functools.partial(_flash_kernel, tq=tq),
        grid=(batch, MLA_HEADS, nq),
        in_specs=[pl.BlockSpec((None, tq, QK_DIM), lambda b, h, i: (h, b * nq + i, 0)),
                  pl.BlockSpec((None, seq, QK_DIM), lambda b, h, i: (h, b, 0)),
                  pl.BlockSpec((None, seq, V_DIM), lambda b, h, i: (h, b, 0))],
        out_specs=pl.BlockSpec((tq, V_DIM), lambda b, h, i: (b * nq + i, h)),
        out_shape=jax.ShapeDtypeStruct((mp, MLA_HEADS * V_DIM), BF16),
        compiler_params=_params("parallel", "parallel", "arbitrary"),
        name="flash_prompt",
    )(q, k, v)


def _absorb_kernel(q_ref, w_ref, o_ref):
    q = q_ref[...]
    qa = jnp.dot(q[:, 0:NOPE_DIM], w_ref[...], preferred_element_type=F32)
    o_ref[:, 0:KV_LORA] = qa.astype(BF16)
    o_ref[:, KV_LORA:KV_LORA + ROPE_DIM] = q[:, NOPE_DIM:QK_DIM]


def _absorb(q, w_uk_t, mp, ms):
    assert mp % ms == 0
    return pl.pallas_call(
        _absorb_kernel,
        grid=(MLA_HEADS,),
        in_specs=[pl.BlockSpec((None, ms, QK_DIM), lambda h: (h, mp // ms, 0)),
                  pl.BlockSpec((None, NOPE_DIM, KV_LORA), lambda h: (h, 0, 0))],
        out_specs=pl.BlockSpec((None, ms, KV_LORA + ROPE_DIM), lambda h: (h, 0, 0)),
        out_shape=jax.ShapeDtypeStruct((MLA_HEADS, ms, KV_LORA + ROPE_DIM), BF16),
        compiler_params=_params("parallel"),
        name="absorb_q",
    )(q, w_uk_t)


def _split3(x):
    hi = x.astype(BF16)
    r1 = x - hi.astype(F32)
    mid = r1.astype(BF16)
    lo = (r1 - mid.astype(F32)).astype(BF16)
    return hi, mid, lo


def _paged_kernel(pt_ref, q_ref, e_ref, *refs, pages_per_step, n_chunks, dec_seq, page):
    g = pages_per_step
    lat_refs = refs[0:g]
    ksc_refs = refs[g:2 * g]
    new_rows_ref, new_ks_ref, o_ref, m_scr, l_scr, acc_scr, tail_lat, tail_ks = refs[2 * g:]
    b = pl.program_id(0)
    c = pl.program_id(1)
    rows = q_ref.shape[0]
    nt_dims = (((1,), (1,)), ((), ()))

    @pl.when(jnp.logical_and(b == 0, c == 0))
    def _():
        tail_lat[...] = jnp.zeros(tail_lat.shape, F32)
        tail_ks[...] = jnp.zeros(tail_ks.shape, F32)

    @pl.when(c == 0)
    def _():
        m_scr[...] = jnp.full(m_scr.shape, NEG_BIG, F32)
        l_scr[...] = jnp.zeros(l_scr.shape, F32)
        acc_scr[...] = jnp.zeros(acc_scr.shape, F32)

    q = q_ref[...]
    e = e_ref[...]

    def select_heads(ks, dims):
        out = None
        for part in _split3(ks):
            t = lax.dot_general(e, part, dims, preferred_element_type=F32)
            out = t if out is None else out + t
        return out

    def softmax_update(s, pv):
        m = m_scr[...]
        m_new = jnp.maximum(m, jnp.max(s, axis=-1, keepdims=True))
        p = jnp.exp(s - m_new[:, 0:1])
        corr = jnp.exp(m - m_new)
        l_scr[...] = l_scr[...] * corr + jnp.sum(p, axis=-1, keepdims=True)
        acc_scr[...] = acc_scr[...] * corr[:, 0:1] + pv(p.astype(BF16))
        m_scr[...] = m_new

    lat_t = jnp.concatenate([r[...].astype(BF16) for r in lat_refs], axis=1)
    ks_t = jnp.concatenate([r[...] for r in ksc_refs], axis=1)
    s = jnp.dot(q, lat_t, preferred_element_type=F32) * select_heads(ks_t, (((1,), (0,)), ((), ())))
    softmax_update(s, lambda pb: lax.dot_general(pb, lat_t[0:KV_LORA, :], nt_dims, preferred_element_type=F32))

    @pl.when(c == n_chunks - 1)
    def _():
        tail_lat[0:dec_seq, :] = new_rows_ref[...]
        tail_ks[0:dec_seq, :] = new_ks_ref[...]
        tl = tail_lat[...].astype(BF16)
        st = lax.dot_general(q, tl, nt_dims, preferred_element_type=F32) * select_heads(tail_ks[...], nt_dims)
        row = lax.broadcasted_iota(jnp.int32, (rows, page), 0)
        col = lax.broadcasted_iota(jnp.int32, (rows, page), 1)
        st = jnp.where(col <= (row % dec_seq), st, NEG_BIG)
        softmax_update(st, lambda pb: jnp.dot(pb, tl[:, 0:KV_LORA], preferred_element_type=F32))
        o_ref[...] = acc_scr[...] / l_scr[:, 0:1]


def _paged_attention(layer, page_table, q_s, e_sel, cache_latent, cache_kscale, new_rows, new_ks, pages_per_step):
    bs, rows, _ = q_s.shape
    n_pages = page_table.shape[1]
    page = cache_latent.shape[3]
    dec_seq = new_rows.shape[1]
    g = pages_per_step
    assert n_pages % g == 0
    nc = n_pages // g
    lat_w = KV_LORA + ROPE_DIM

    def lat_map(k):
        return lambda b, c, pt: (layer, pt[b, c * g + k], 0, 0)

    in_specs = [pl.BlockSpec((None, rows, lat_w), lambda b, c, pt: (b, 0, 0)),
                pl.BlockSpec((rows, MLA_HEADS), lambda b, c, pt: (0, 0))]
    in_specs += [pl.BlockSpec((None, None, lat_w, page), lat_map(k)) for k in range(g)]
    in_specs += [pl.BlockSpec((None, None, MLA_HEADS, page), lat_map(k)) for k in range(g)]
    in_specs += [pl.BlockSpec((None, dec_seq, lat_w), lambda b, c, pt: (b, 0, 0)),
                 pl.BlockSpec((None, dec_seq, MLA_HEADS), lambda b, c, pt: (b, 0, 0))]
    grid_spec = pltpu.PrefetchScalarGridSpec(
        num_scalar_prefetch=1,
        grid=(bs, nc),
        in_specs=in_specs,
        out_specs=pl.BlockSpec((None, rows, KV_LORA), lambda b, c, pt: (b, 0, 0)),
        scratch_shapes=[pltpu.VMEM((rows, LANES), F32), pltpu.VMEM((rows, LANES), F32),
                        pltpu.VMEM((rows, KV_LORA), F32),
                        pltpu.VMEM((page, lat_w), F32), pltpu.VMEM((page, MLA_HEADS), F32)])
    return pl.pallas_call(
        functools.partial(_paged_kernel, pages_per_step=g, n_chunks=nc, dec_seq=dec_seq, page=page),
        grid_spec=grid_spec,
        out_shape=jax.ShapeDtypeStruct((bs, rows, KV_LORA), F32),
        compiler_params=_params("arbitrary", "arbitrary"),
        name="paged_attention",
    )(page_table, q_s, e_sel, *([cache_latent] * g), *([cache_kscale] * g), new_rows, new_ks)


def _headmm_kernel(x_ref, w_ref, o_ref):
    o_ref[...] = jnp.dot(x_ref[...], w_ref[...], preferred_element_type=F32).astype(o_ref.dtype)


def _sample_vproj(o_lat_h, w_uv_h):
    h, ms, k = o_lat_h.shape
    return pl.pallas_call(
        _headmm_kernel,
        grid=(h,),
        in_specs=[pl.BlockSpec((None, ms, k), lambda i: (i, 0, 0)),
                  pl.BlockSpec((None, k, V_DIM), lambda i: (i, 0, 0))],
        out_specs=pl.BlockSpec((ms, V_DIM), lambda i: (0, i)),
        out_shape=jax.ShapeDtypeStruct((ms, h * V_DIM), BF16),
        compiler_params=_params("parallel"),
        name="sample_vproj",
    )(o_lat_h, w_uv_h)


def _ln_silu(y, g, b):
    yc = y - jnp.mean(y, axis=-1, keepdims=True)
    var = jnp.mean(yc * yc, axis=-1, keepdims=True)
    return _silu(yc * lax.rsqrt(var + EPS) * g + b)


CONV_HALO = 32
CONV_ROWS = 32


def _conv_prompt_kernel(a_ref, g_ref, w_ref, bdw_ref, gln_ref, bln_ref, yc_ref, st_ref, buf, *, t_rows, nt):
    t = pl.program_id(1)

    @pl.when(t == 0)
    def _():
        buf[0:CONV_HALO, :] = jnp.zeros((CONV_HALO, CONV_CH), F32)

    @pl.when(t > 0)
    def _():
        buf[0:CONV_HALO, :] = buf[t_rows:t_rows + CONV_HALO, :]

    buf[CONV_HALO:CONV_HALO + t_rows, :] = a_ref[...] * _sigmoid(g_ref[...])
    off = CONV_HALO - (CONV_K - 1)
    for r in range(t_rows // CONV_ROWS):
        base = r * CONV_ROWS
        acc = jnp.broadcast_to(bdw_ref[...], (CONV_ROWS, CONV_CH))
        for j in range(CONV_K):
            acc = acc + buf[base + off + j: base + off + j + CONV_ROWS, :] * w_ref[j:j + 1, :]
        yc_ref[base:base + CONV_ROWS, :] = _ln_silu(acc, gln_ref[...], bln_ref[...]).astype(BF16)

    @pl.when(t == nt - 1)
    def _():
        st_ref[...] = buf[t_rows + off:t_rows + CONV_HALO, :]


def _conv_prompt(z, w_dw, b_dw, g_ln, b_ln, batch, seq):
    t_rows = _pick(seq, 256, CONV_ROWS)
    nt = seq // t_rows
    mp = batch * seq
    row = lambda b, t: (0, 0)
    return pl.pallas_call(
        functools.partial(_conv_prompt_kernel, t_rows=t_rows, nt=nt),
        grid=(batch, nt),
        in_specs=[pl.BlockSpec((t_rows, CONV_CH), lambda b, t: (b * nt + t, Z_CONV_A // CONV_CH)),
                  pl.BlockSpec((t_rows, CONV_CH), lambda b, t: (b * nt + t, Z_CONV_G // CONV_CH)),
                  pl.BlockSpec((CONV_HALO, CONV_CH), row),
                  pl.BlockSpec((1, CONV_CH), row), pl.BlockSpec((1, CONV_CH), row), pl.BlockSpec((1, CONV_CH), row)],
        out_specs=[pl.BlockSpec((t_rows, CONV_CH), lambda b, t: (b * nt + t, 0)),
                   pl.BlockSpec((None, CONV_K - 1, CONV_CH), lambda b, t: (b, 0, 0))],
        out_shape=[jax.ShapeDtypeStruct((mp, CONV_CH), BF16),
                   jax.ShapeDtypeStruct((batch, CONV_K - 1, CONV_CH), F32)],
        scratch_shapes=[pltpu.VMEM((t_rows + CONV_HALO, CONV_CH), F32)],
        compiler_params=_params("parallel", "arbitrary"),
        name="conv_prompt",
    )(z, z, w_dw, b_dw, g_ln, b_ln)


def _conv_sample_kernel(a_ref, g_ref, st_ref, w_ref, bdw_ref, gln_ref, bln_ref, yc_ref, nst_ref, buf, ybuf,
                        *, n_seq, dec_seq):
    hist = CONV_K - 1
    a = a_ref[...] * _sigmoid(g_ref[...])
    w = w_ref[0:CONV_K, :]
    for s in range(n_seq):
        buf[s, 0:hist, :] = st_ref[s]
        buf[s, hist:hist + dec_seq, :] = a[s * dec_seq:(s + 1) * dec_seq, :]
    for s in range(n_seq):
        for t in range(dec_seq):
            y = jnp.sum(buf[s, t:t + CONV_K, :] * w, axis=0, keepdims=True) + bdw_ref[...]
            ybuf[s * dec_seq + t: s * dec_seq + t + 1, :] = y
        nst_ref[s] = buf[s, dec_seq:dec_seq + hist, :]
    yc_ref[...] = _ln_silu(ybuf[...], gln_ref[...], bln_ref[...]).astype(BF16)


def _conv_sample(layer, z, state_conv, w_dw, b_dw, g_ln, b_ln, mp, bs, dec_seq):
    n_seq = _pick(bs, 8, 1)
    rows = n_seq * dec_seq
    assert rows % 16 == 0 and mp % rows == 0
    hist = CONV_K - 1
    row = lambda i: (0, 0)
    return pl.pallas_call(
        functools.partial(_conv_sample_kernel, n_seq=n_seq, dec_seq=dec_seq),
        grid=(bs // n_seq,),
        in_specs=[pl.BlockSpec((rows, CONV_CH), lambda i: (mp // rows + i, Z_CONV_A // CONV_CH)),
                  pl.BlockSpec((rows, CONV_CH), lambda i: (mp // rows + i, Z_CONV_G // CONV_CH)),
                  pl.BlockSpec((None, n_seq, hist, CONV_CH), lambda i: (layer, i, 0, 0)),
                  pl.BlockSpec((CONV_HALO, CONV_CH), row),
                  pl.BlockSpec((1, CONV_CH), row), pl.BlockSpec((1, CONV_CH), row), pl.BlockSpec((1, CONV_CH), row)],
        out_specs=[pl.BlockSpec((rows, CONV_CH), lambda i: (i, 0)),
                   pl.BlockSpec((n_seq, hist, CONV_CH), lambda i: (i, 0, 0))],
        out_shape=[jax.ShapeDtypeStruct((bs * dec_seq, CONV_CH), BF16),
                   jax.ShapeDtypeStruct((bs, hist, CONV_CH), F32)],
        scratch_shapes=[pltpu.VMEM((n_seq, hist + SUBLANES + 2, CONV_CH), F32), pltpu.VMEM((rows, CONV_CH), F32)],
        compiler_params=_params("parallel"),
        name="conv_sample",
    )(z, z, state_conv, w_dw, b_dw, g_ln, b_ln)


def _hgrn_gates(hq, hf, hi, lb):
    log_lb = jnp.log(jnp.maximum(lb, LB_FLOOR))
    log_1m = jnp.log1p(-lb)
    log_sig = jnp.minimum(hf, 0.0) - jnp.log1p(jnp.exp(-jnp.abs(hf)))
    bb = log_1m + log_sig
    logf = jnp.maximum(log_lb, bb) + jnp.log1p(jnp.exp(-jnp.abs(log_lb - bb)))
    k = 1.0 - jnp.exp(logf)
    return _silu(hq), k, _silu(hi), logf


def _chunk_cumsum(bd, logf):
    out = None
    for part in _split3(logf):
        t = jnp.dot(bd, part, preferred_element_type=F32)
        out = t if out is None else out + t
    return out


def _intra_weights(q, k, cum, valid_fn, n_rows):
    lane = lax.broadcasted_iota(jnp.int32, (SUBLANES, LANES), 1)
    srow = lax.broadcasted_iota(jnp.int32, (SUBLANES, LANES), 0)
    blocks = [jnp.zeros((SUBLANES, LANES), F32) for _ in range(n_rows // SUBLANES)]
    for t in range(n_rows):
        qt = q[t:t + 1, :]
        ct = cum[t:t + 1, :]
        for rb in range(n_rows // SUBLANES):
            mask = valid_fn(srow + rb * SUBLANES, t, rb)
            if mask is None:
                continue
            sl = slice(rb * SUBLANES, (rb + 1) * SUBLANES)
            d = ct - cum[sl, :]
            if mask is not True:
                d = jnp.where(mask, d, NEG_BIG)
            w = jnp.sum(jnp.exp(d) * (qt * k[sl, :]), axis=-1, keepdims=True)
            blocks[rb] = jnp.where(lane == t, w, blocks[rb])
    return jnp.concatenate(blocks, axis=0)


def _chunk_attention(qc, kc, cc):
    n = qc.shape[0]
    nb = n // SUBLANES
    lane = lax.broadcasted_iota(jnp.int32, (SUBLANES, LANES), 1)
    srow = lax.broadcasted_iota(jnp.int32, (SUBLANES, LANES), 0)
    blocks = []
    for ib in range(nb):
        sl = slice(ib * SUBLANES, (ib + 1) * SUBLANES)
        cb, kb = cc[sl, :], kc[sl, :]
        blk = jnp.zeros((SUBLANES, LANES), F32)
        for tt in range(SUBLANES):
            t = ib * SUBLANES + tt
            d = jnp.where(srow <= tt, cc[t:t + 1, :] - cb, NEG_BIG)
            w = jnp.sum(jnp.exp(d) * (qc[t:t + 1, :] * kb), axis=-1, keepdims=True)
            blk = jnp.where(lane == t, w, blk)
        blocks.append(blk)
    att = jnp.concatenate(blocks, axis=0)
    if nb == 1:
        return att
    ks_parts, qs_parts = [], []
    for ib in range(1, nb):
        lo = ib * SUBLANES
        ref = cc[lo - 1:lo, :]
        kpart = kc[0:lo, :] * jnp.exp(ref - cc[0:lo, :])
        qpart = qc[lo:lo + SUBLANES, :] * jnp.exp(cc[lo:lo + SUBLANES, :] - ref)
        ks_parts.append(jnp.concatenate([kpart, jnp.zeros((n - lo, LANES), F32)], axis=0))
        qs_parts.append(jnp.concatenate([jnp.zeros((lo, LANES), F32), qpart,
                                         jnp.zeros((LANES - lo - SUBLANES, LANES), F32)], axis=0))
    ks = jnp.concatenate(ks_parts, axis=1).astype(BF16)
    qs = jnp.concatenate(qs_parts, axis=1).astype(BF16)
    return att + lax.dot_general(ks, qs, (((1,), (1,)), ((), ())), preferred_element_type=F32)


def _hgrn_prompt_kernel(hq_ref, hf_ref, hi_ref, hg_ref, lb_ref, go_ref, bd_ref, o_ref, s_ref, st_scr,
                        *, t_rows, nt, chunk):
    t = pl.program_id(2)
    tn_dims = (((0,), (0,)), ((), ()))

    @pl.when(t == 0)
    def _():
        st_scr[...] = jnp.zeros(st_scr.shape, F32)

    q, k, v, logf = _hgrn_gates(hq_ref[...], hf_ref[...], hi_ref[...], lb_ref[...])
    cum = _chunk_cumsum(bd_ref[...], logf)
    st = st_scr[...]
    outs = []
    for ci in range(t_rows // chunk):
        sl = slice(ci * chunk, (ci + 1) * chunk)
        qc, kc, cc = q[sl, :], k[sl, :], cum[sl, :]
        vb = v[sl, :].astype(BF16)
        att_t = _chunk_attention(qc, kc, cc)
        intra = lax.dot_general(att_t.astype(BF16), vb, tn_dims, preferred_element_type=F32)
        last = cc[chunk - 1:chunk, :]
        kv = lax.dot_general(vb, (kc * jnp.exp(last - cc)).astype(BF16), tn_dims, preferred_element_type=F32)
        inter = lax.dot_general((qc * jnp.exp(cc)).astype(BF16), st.astype(BF16),
                                (((1,), (1,)), ((), ())), preferred_element_type=F32)
        outs.append(inter + intra[0:chunk, :])
        st = st * jnp.exp(last) + kv
    st_scr[...] = st
    o = jnp.concatenate(outs, axis=0)
    o_ref[...] = (_rms_rows(o, go_ref[...]) * _sigmoid(hg_ref[...])).astype(BF16)

    @pl.when(t == nt - 1)
    def _():
        s_ref[...] = st_scr[...].T


def _block_tri(n, block):
    r = jnp.arange(n)
    return ((r[:, None] >= r[None, :]) & ((r[:, None] // block) == (r[None, :] // block))).astype(BF16)


def _hgrn_prompt(z, lb, g_o, batch, seq):
    chunk = HGRN_CHUNK if seq % HGRN_CHUNK == 0 else seq
    assert chunk % SUBLANES == 0 and chunk <= LANES
    t_rows = _pick(seq, 256, chunk)
    nt = seq // t_rows
    mp = batch * seq
    bd = _block_tri(t_rows, chunk)

    def zmap(off):
        return lambda b, h, t: (b * nt + t, off // HGRN_DK + h)

    return pl.pallas_call(
        functools.partial(_hgrn_prompt_kernel, t_rows=t_rows, nt=nt, chunk=chunk),
        grid=(batch, HGRN_HEADS, nt),
        in_specs=[pl.BlockSpec((t_rows, HGRN_DK), zmap(Z_HQ)),
                  pl.BlockSpec((t_rows, HGRN_DK), zmap(Z_HF)),
                  pl.BlockSpec((t_rows, HGRN_DK), zmap(Z_HI)),
                  pl.BlockSpec((t_rows, HGRN_DK), zmap(Z_HG)),
                  pl.BlockSpec((1, HGRN_DK), lambda b, h, t: (0, h)),
                  pl.BlockSpec((1, HGRN_DV), lambda b, h, t: (0, 0)),
                  pl.BlockSpec((t_rows, t_rows), lambda b, h, t: (0, 0))],
        out_specs=[pl.BlockSpec((t_rows, HGRN_DV), lambda b, h, t: (b * nt + t, h)),
                   pl.BlockSpec((None, None, HGRN_DK, HGRN_DV), lambda b, h, t: (b, h, 0, 0))],
        out_shape=[jax.ShapeDtypeStruct((mp, HGRN_HEADS * HGRN_DV), BF16),
                   jax.ShapeDtypeStruct((batch, HGRN_HEADS, HGRN_DK, HGRN_DV), F32)],
        scratch_shapes=[pltpu.VMEM((HGRN_DV, HGRN_DK), F32)],
        compiler_params=_params("parallel", "parallel", "arbitrary"),
        name="hgrn_prompt",
    )(z, z, z, z, lb, g_o, bd)


def _hgrn_sample_kernel(hq_ref, hf_ref, hi_ref, hg_ref, lb_ref, go_ref, bd_ref, s0_ref, o_ref, s_ref,
                        *, n_seq, dec_seq):
    rows = n_seq * dec_seq
    q, k, v, logf = _hgrn_gates(hq_ref[...], hf_ref[...], hi_ref[...], lb_ref[...])
    cum = _chunk_cumsum(bd_ref[...], logf)
    vb = v.astype(BF16)
    qe = (q * jnp.exp(cum)).astype(BF16)

    def valid(s_idx, t, rb):
        lo = (t // dec_seq) * dec_seq
        if lo // SUBLANES != rb:
            return None
        return jnp.logical_and(s_idx >= lo, s_idx <= t)

    att_t = _intra_weights(q, k, cum, valid, rows)
    intra = lax.dot_general(att_t.astype(BF16), vb, (((0,), (0,)), ((), ())),
                            preferred_element_type=F32)
    row = lax.broadcasted_iota(jnp.int32, (rows, HGRN_DK), 0)
    o = intra[0:rows, :]
    for s in range(n_seq):
        in_seq = (row // dec_seq) == s
        st = s0_ref[s].T
        inter = lax.dot_general(qe, st.astype(BF16), (((1,), (1,)), ((), ())), preferred_element_type=F32)
        o = o + jnp.where(in_seq, inter, 0.0)
        last = cum[(s + 1) * dec_seq - 1:(s + 1) * dec_seq, :]
        kt = (k * jnp.exp(jnp.where(in_seq, last - cum, NEG_BIG))).astype(BF16)
        st_new = st * jnp.exp(last) + lax.dot_general(vb, kt, (((0,), (0,)), ((), ())),
                                                      preferred_element_type=F32)
        s_ref[s] = st_new.T
    o_ref[...] = (_rms_rows(o, go_ref[...]) * _sigmoid(hg_ref[...])).astype(BF16)


def _hgrn_sample(layer, z, state_hgrn, lb, g_o, mp, bs, dec_seq):
    n_seq = _pick(bs, max(1, 16 // dec_seq), 1)
    rows = n_seq * dec_seq
    assert rows % 16 == 0 and mp % rows == 0 and SUBLANES % dec_seq == 0
    bd = _block_tri(rows, dec_seq)

    def zmap(off):
        return lambda i, h: (mp // rows + i, off // HGRN_DK + h)

    return pl.pallas_call(
        functools.partial(_hgrn_sample_kernel, n_seq=n_seq, dec_seq=dec_seq),
        grid=(bs // n_seq, HGRN_HEADS),
        in_specs=[pl.BlockSpec((rows, HGRN_DK), zmap(Z_HQ)),
                  pl.BlockSpec((rows, HGRN_DK), zmap(Z_HF)),
                  pl.BlockSpec((rows, HGRN_DK), zmap(Z_HI)),
                  pl.BlockSpec((rows, HGRN_DK), zmap(Z_HG)),
                  pl.BlockSpec((1, HGRN_DK), lambda i, h: (0, h)),
                  pl.BlockSpec((1, HGRN_DV), lambda i, h: (0, 0)),
                  pl.BlockSpec((rows, rows), lambda i, h: (0, 0)),
                  pl.BlockSpec((None, n_seq, None, HGRN_DK, HGRN_DV), lambda i, h: (layer, i, h, 0, 0))],
        out_specs=[pl.BlockSpec((rows, HGRN_DV), lambda i, h: (i, h)),
                   pl.BlockSpec((n_seq, None, HGRN_DK, HGRN_DV), lambda i, h: (i, h, 0, 0))],
        out_shape=[jax.ShapeDtypeStruct((bs * dec_seq, HGRN_HEADS * HGRN_DV), BF16),
                   jax.ShapeDtypeStruct((bs, HGRN_HEADS, HGRN_DK, HGRN_DV), F32)],
        compiler_params=_params("parallel", "parallel"),
        name="hgrn_sample",
    )(z, z, z, z, lb, g_o, bd, state_hgrn)


def _mix_kernel(om_ref, yc_ref, oh_ref, wm_ref, wc_ref, wh_ref, g0_ref, g1_ref, g2_ref, mix_ref):
    ym = jnp.dot(om_ref[...], wm_ref[...], preferred_element_type=F32)
    yc = jnp.dot(yc_ref[...], wc_ref[...], preferred_element_type=F32)
    yh = jnp.dot(oh_ref[...], wh_ref[...], preferred_element_type=F32)
    mix = _sigmoid(g0_ref[...]) * ym + _sigmoid(g1_ref[...]) * yc + _sigmoid(g2_ref[...]) * yh
    mix_ref[...] = mix.astype(BF16)


def _mix(o_mla, yc, o_h, w_o_mla, w_pw, w_o_hgrn, z):
    m = o_mla.shape[0]
    bm = _pick(m, 768, 16)
    bn = 512

    def gmap(off):
        return lambda i, j: (i, off // bn + j)

    return pl.pallas_call(
        _mix_kernel,
        grid=(m // bm, D_MODEL // bn),
        in_specs=[pl.BlockSpec((bm, MLA_HEADS * V_DIM), lambda i, j: (i, 0)),
                  pl.BlockSpec((bm, CONV_CH), lambda i, j: (i, 0)),
                  pl.BlockSpec((bm, HGRN_HEADS * HGRN_DV), lambda i, j: (i, 0)),
                  pl.BlockSpec((MLA_HEADS * V_DIM, bn), lambda i, j: (0, j)),
                  pl.BlockSpec((CONV_CH, bn), lambda i, j: (0, j)),
                  pl.BlockSpec((HGRN_HEADS * HGRN_DV, bn), lambda i, j: (0, j)),
                  pl.BlockSpec((bm, bn), gmap(Z_G0)),
                  pl.BlockSpec((bm, bn), gmap(Z_G1)),
                  pl.BlockSpec((bm, bn), gmap(Z_G2))],
        out_specs=pl.BlockSpec((bm, bn), lambda i, j: (i, j)),
        out_shape=jax.ShapeDtypeStruct((m, D_MODEL), BF16),
        compiler_params=_params("parallel", "arbitrary"),
        name="branch_mix",
    )(o_mla, yc, o_h, w_o_mla, w_pw, w_o_hgrn, z, z, z)


def _resmm_kernel(a_ref, w_ref, x_ref, o_ref):
    o_ref[...] = x_ref[...] + jnp.dot(a_ref[...], w_ref[...], preferred_element_type=F32)


def _residual_matmul(a, w, x, bm_target, bn):
    m, k = a.shape
    n = w.shape[1]
    bm = _pick(m, bm_target, 16)
    return pl.pallas_call(
        _resmm_kernel,
        grid=(m // bm, n // bn),
        in_specs=[pl.BlockSpec((bm, k), lambda i, j: (i, 0)),
                  pl.BlockSpec((k, bn), lambda i, j: (0, j)),
                  pl.BlockSpec((bm, bn), lambda i, j: (i, j))],
        out_specs=pl.BlockSpec((bm, bn), lambda i, j: (i, j)),
        out_shape=jax.ShapeDtypeStruct((m, n), F32),
        compiler_params=_params("parallel", "arbitrary"),
        name="residual_matmul",
    )(a, w, x)


def _ffn_up_kernel(x_ref, g_ref, wg_ref, wu_ref, o_ref, h_scr):
    @pl.when(pl.program_id(1) == 0)
    def _():
        h_scr[...] = _rms_rows(x_ref[...], g_ref[...]).astype(BF16)

    h = h_scr[...]
    gate = jnp.dot(h, wg_ref[...], preferred_element_type=F32)
    up = jnp.dot(h, wu_ref[...], preferred_element_type=F32)
    o_ref[...] = (_silu(gate) * up).astype(BF16)


def _ffn_up(x, g, w_gu):
    m, k = x.shape
    bm = _pick(m, 1536, 16)
    bn = 512
    nj = D_FF // bn
    return pl.pallas_call(
        _ffn_up_kernel,
        grid=(m // bm, nj),
        in_specs=[pl.BlockSpec((bm, k), lambda i, j: (i, 0)),
                  pl.BlockSpec((1, k), lambda i, j: (0, 0)),
                  pl.BlockSpec((k, bn), lambda i, j: (0, j)),
                  pl.BlockSpec((k, bn), lambda i, j: (0, nj + j))],
        out_specs=pl.BlockSpec((bm, bn), lambda i, j: (i, j)),
        out_shape=jax.ShapeDtypeStruct((m, D_FF), BF16),
        scratch_shapes=[pltpu.VMEM((bm, k), BF16)],
        compiler_params=_params("parallel", "arbitrary"),
        name="ffn_up",
    )(x, g, w_gu, w_gu)


def _ple_kernel(x_ref, g_ref, wg_ref, p_ref, wp_ref, xr_ref, o_ref, h_scr, p_scr):
    @pl.when(pl.program_id(1) == 0)
    def _():
        h_scr[...] = _rms_rows(x_ref[...], g_ref[...]).astype(BF16)
        p_scr[...] = p_ref[...].astype(BF16)

    gate = _sigmoid(jnp.dot(h_scr[...], wg_ref[...], preferred_element_type=F32))
    proj = jnp.dot(p_scr[...], wp_ref[...], preferred_element_type=F32)
    o_ref[...] = xr_ref[...] + proj * gate


def _ple(x, g, w_gate, p, w_proj):
    m, k = x.shape
    bm = _pick(m, 768, 16)
    bn = 512
    return pl.pallas_call(
        _ple_kernel,
        grid=(m // bm, D_MODEL // bn),
        in_specs=[pl.BlockSpec((bm, k), lambda i, j: (i, 0)),
                  pl.BlockSpec((1, k), lambda i, j: (0, 0)),
                  pl.BlockSpec((k, bn), lambda i, j: (0, j)),
                  pl.BlockSpec((bm, PLE_DIM), lambda i, j: (i, 0)),
                  pl.BlockSpec((PLE_DIM, bn), lambda i, j: (0, j)),
                  pl.BlockSpec((bm, bn), lambda i, j: (i, j))],
        out_specs=pl.BlockSpec((bm, bn), lambda i, j: (i, j)),
        out_shape=jax.ShapeDtypeStruct((m, D_MODEL), F32),
        scratch_shapes=[pltpu.VMEM((bm, k), BF16), pltpu.VMEM((bm, PLE_DIM), BF16)],
        compiler_params=_params("parallel", "arbitrary"),
        name="ple",
    )(x, g, w_gate, p, w_proj, x)


def _rope_tables(pos):
    half = ROPE_DIM // 2
    inv = ROPE_THETA ** (-jnp.arange(half, dtype=F32) / half)
    ang = pos.astype(F32)[:, None] * inv[None, :]
    cos, sin = jnp.cos(ang), jnp.sin(ang)
    reps = LANES // ROPE_DIM
    cos_t = jnp.tile(jnp.concatenate([cos, cos], axis=-1), (1, reps))
    sin_t = jnp.tile(jnp.concatenate([-sin, sin], axis=-1), (1, reps))
    return cos_t, sin_t


def _row(v):
    return v.reshape(1, -1).astype(F32)


def kernel(x_prompt, x_sample, cache_latent, cache_kscale, state_conv, state_hgrn, page_table, p_prompt, p_sample, g_mix, w_in, g_q, w_uq, g_kv, g_qk, g_kk, w_uk, w_uv, w_o_mla, w_dw, b_dw, g_cln, b_cln, w_pw, lb_logits, g_hgrn, w_o_hgrn, w_out, g_ffn, w_gu, w_down, g_ple, w_ple_gate, w_ple_proj):
    depth = w_in.shape[0]
    batch, seq, _ = x_prompt.shape
    bs, dec_seq, _ = x_sample.shape
    mp, ms = batch * seq, bs * dec_seq
    n_pages, page = page_table.shape[1], cache_latent.shape[2]
    past = n_pages * page

    lb_p = jax.nn.softmax(lb_logits.astype(F32), axis=0)
    lb_all = jnp.cumsum(lb_p, axis=0) - lb_p[0:1]

    cos_p, sin_p = _rope_tables(jnp.arange(seq, dtype=jnp.int32))
    cos_s, sin_s = _rope_tables(past + jnp.arange(dec_seq, dtype=jnp.int32))
    cos_t = jnp.concatenate([jnp.tile(cos_p, (batch, 1)), jnp.tile(cos_s, (bs, 1))], axis=0)
    sin_t = jnp.concatenate([jnp.tile(sin_p, (batch, 1)), jnp.tile(sin_s, (bs, 1))], axis=0)

    rows_q = MLA_HEADS * dec_seq
    e_sel = (jnp.arange(rows_q)[:, None] // dec_seq == jnp.arange(MLA_HEADS)[None, :]).astype(BF16)
    pages_per_step = _pick(n_pages, 32, 1)
    cache_latent_t = jnp.swapaxes(cache_latent, 2, 3)
    cache_kscale_t = jnp.swapaxes(cache_kscale, 2, 3)

    x = jnp.concatenate([x_prompt.reshape(mp, D_MODEL), x_sample.reshape(ms, D_MODEL)], axis=0)
    lat_p, ks_p, conv_p, hg_p, lat_s, ks_s, conv_s, hg_s = ([] for _ in range(8))
    for i in range(depth):
        wi = w_in[i]
        w_in_r = jnp.concatenate(
            [wi[:, 0:Q_LORA + KV_LORA], wi[:, Q_LORA + KV_LORA + ROPE_DIM:], wi[:, Q_LORA + KV_LORA:Q_LORA + KV_LORA + ROPE_DIM],
             jnp.zeros((D_MODEL, Z_COLS - wi.shape[1]), wi.dtype)], axis=1).astype(BF16)
        wq = w_uq[i].reshape(Q_LORA, MLA_HEADS, QK_DIM)
        w_uq_r = jnp.concatenate([wq[:, :, 0:NOPE_DIM].reshape(Q_LORA, -1), wq[:, :, NOPE_DIM:].reshape(Q_LORA, -1)],
                                 axis=1).astype(BF16)
        w_uk2 = w_uk[i].reshape(KV_LORA, -1).astype(BF16)
        w_uv2 = w_uv[i].reshape(KV_LORA, -1).astype(BF16)
        w_uk_t = jnp.transpose(w_uk[i], (1, 2, 0)).astype(BF16)
        w_uv_h = jnp.transpose(w_uv[i], (1, 0, 2)).astype(BF16)
        gqkn = _row(g_qk[i, 0:NOPE_DIM])
        gkkn = _row(g_kk[i, 0:NOPE_DIM])
        gqkr = _row(jnp.tile(g_qk[i, NOPE_DIM:], LANES // ROPE_DIM))
        gkkr = _row(jnp.concatenate([g_kk[i, NOPE_DIM:], jnp.zeros((LANES - ROPE_DIM,), F32)]))
        w_dw_p = jnp.concatenate([w_dw[i], jnp.zeros((CONV_HALO - CONV_K, CONV_CH), F32)], axis=0)
        lb = _row(lb_all[i])

        z = _inproj(x, _row(g_mix[i]), w_in_r)
        q_all = _qproj(z, _row(g_q[i]), w_uq_r, gqkn, gkkn, gqkr, cos_t, sin_t)
        rows, kscale, k_all, v_all = _kvproj(z, _row(g_kv[i]), gkkr, w_uk2, w_uv2, cos_t, sin_t)

        o_mla_p = _flash(q_all, k_all, v_all, batch, seq)
        q_s = _absorb(q_all, w_uk_t, mp, ms)
        q_s = q_s.reshape(MLA_HEADS, bs, dec_seq, -1).transpose(1, 0, 2, 3).reshape(bs, rows_q, -1)
        new_rows = rows[mp:].reshape(bs, dec_seq, -1)
        new_ks = kscale[mp:].reshape(bs, dec_seq, MLA_HEADS)
        o_lat = _paged_attention(i, page_table, q_s, e_sel, cache_latent_t, cache_kscale_t, new_rows, new_ks,
                                 pages_per_step)
        o_lat_h = o_lat.reshape(bs, MLA_HEADS, dec_seq, KV_LORA).transpose(1, 0, 2, 3)
        o_lat_h = o_lat_h.reshape(MLA_HEADS, ms, KV_LORA).astype(BF16)
        o_mla_s = _sample_vproj(o_lat_h, w_uv_h)
        o_mla = jnp.concatenate([o_mla_p, o_mla_s], axis=0)

        yc_p, conv_new_p = _conv_prompt(z, w_dw_p, _row(b_dw[i]), _row(g_cln[i]), _row(b_cln[i]), batch, seq)
        yc_s, conv_new_s = _conv_sample(i, z, state_conv, w_dw_p, _row(b_dw[i]), _row(g_cln[i]), _row(b_cln[i]),
                                        mp, bs, dec_seq)
        yc = jnp.concatenate([yc_p, yc_s], axis=0)

        oh_p, s_new_p = _hgrn_prompt(z, lb, _row(g_hgrn[i]), batch, seq)
        oh_s, s_new_s = _hgrn_sample(i, z, state_hgrn, lb, _row(g_hgrn[i]), mp, bs, dec_seq)
        o_h = jnp.concatenate([oh_p, oh_s], axis=0)

        mix = _mix(o_mla, yc, o_h, w_o_mla[i].astype(BF16), w_pw[i].astype(BF16), w_o_hgrn[i].astype(BF16), z)
        x = _residual_matmul(mix, w_out[i].astype(BF16), x, 1536, 512)

        act = _ffn_up(x, _row(g_ffn[i]), w_gu[i].astype(BF16))
        x = _residual_matmul(act, w_down[i].astype(BF16), x, 768, 512)
        p_emb = jnp.concatenate([p_prompt[i].reshape(mp, PLE_DIM), p_sample[i].reshape(ms, PLE_DIM)], axis=0)
        x = _ple(x, _row(g_ple[i]), w_ple_gate[i].astype(BF16), p_emb, w_ple_proj[i].astype(BF16))

        lat_p.append(rows[:mp].reshape(batch, seq, -1))
        ks_p.append(kscale[:mp].reshape(batch, seq, MLA_HEADS))
        conv_p.append(conv_new_p)
        hg_p.append(s_new_p)
        lat_s.append(new_rows)
        ks_s.append(new_ks)
        conv_s.append(conv_new_s)
        hg_s.append(s_new_s)

    return (x[:mp].reshape(batch, seq, D_MODEL), x[mp:].reshape(bs, dec_seq, D_MODEL),
            jnp.stack(lat_p), jnp.stack(ks_p), jnp.stack(conv_p), jnp.stack(hg_p),
            jnp.stack(lat_s), jnp.stack(ks_s), jnp.stack(conv_s), jnp.stack(hg_s))
```

```python
import functools

import jax
import jax.numpy as jnp
from jax import lax
from jax.experimental import pallas as pl
from jax.experimental.pallas import tpu as pltpu

F32 = jnp.float32
BF16 = jnp.bfloat16

D_MODEL = 2048
MLA_HEADS = 16
Q_LORA = 512
KV_LORA = 512
NOPE_DIM = 128
ROPE_DIM = 64
QK_DIM = NOPE_DIM + ROPE_DIM
V_DIM = 128
ROPE_THETA = 10000.0
ATTN_SCALE = QK_DIM ** -0.5
CONV_CH = 1024
CONV_K = 31
HGRN_HEADS = 8
HGRN_DK = 128
HGRN_DV = 128
HGRN_KW = HGRN_HEADS * HGRN_DK
HGRN_CHUNK = 32
LB_FLOOR = 1e-30
D_FF = ((8 * D_MODEL // 3 + 255) // 256) * 256
PLE_DIM = 256
EPS = 1e-6
NEG_BIG = -1e30

LANES = 128
SUBLANES = 8
VMEM_LIMIT_BYTES = 56 * 1024 * 1024

Z_CQ = 0
Z_CKV = 512
Z_CONV_A = 1024
Z_CONV_G = 2048
Z_HQ = 3072
Z_HF = 4096
Z_HI = 5120
Z_HG = 6144
Z_G0 = 7168
Z_G1 = 9216
Z_G2 = 11264
Z_KR = 13312
Z_COLS = 13824


def _pick(total, target, mult=SUBLANES):
    best = None
    for d in range(mult, min(total, target) + 1, mult):
        if total % d == 0:
            best = d
    assert best is not None, (total, target, mult)
    return best


def _params(*sem):
    return pltpu.CompilerParams(dimension_semantics=sem, vmem_limit_bytes=VMEM_LIMIT_BYTES)


def _rms_rows(x, g):
    ms = jnp.mean(x * x, axis=-1, keepdims=True)
    return x * lax.rsqrt(ms + EPS) * g


def _sigmoid(x):
    return 1.0 / (1.0 + jnp.exp(-x))


def _silu(x):
    return x * _sigmoid(x)


def _rope128(x, cos_t, sin_t):
    lane = lax.broadcasted_iota(jnp.int32, x.shape, 1)
    first_half = (lane & (ROPE_DIM // 2)) == 0
    partner = jnp.where(first_half, pltpu.roll(x, LANES - ROPE_DIM // 2, 1), pltpu.roll(x, ROPE_DIM // 2, 1))
    return x * cos_t + partner * sin_t


def _inproj_kernel(x_ref, g_ref, w_ref, z_ref, h_scr):
    @pl.when(pl.program_id(1) == 0)
    def _():
        h_scr[...] = _rms_rows(x_ref[...], g_ref[...]).astype(BF16)

    z_ref[...] = jnp.dot(h_scr[...], w_ref[...], preferred_element_type=F32)


def _inproj(x, g, w):
    m, k = x.shape
    n = w.shape[1]
    bm = _pick(m, 1536)
    bn = _pick(n, 768, LANES)
    return pl.pallas_call(
        _inproj_kernel,
        grid=(m // bm, n // bn),
        in_specs=[pl.BlockSpec((bm, k), lambda i, j: (i, 0)),
                  pl.BlockSpec((1, k), lambda i, j: (0, 0)),
                  pl.BlockSpec((k, bn), lambda i, j: (0, j))],
        out_specs=pl.BlockSpec((bm, bn), lambda i, j: (i, j)),
        out_shape=jax.ShapeDtypeStruct((m, n), F32),
        scratch_shapes=[pltpu.VMEM((bm, k), BF16)],
        compiler_params=_params("parallel", "arbitrary"),
        name="inproj",
    )(x, g, w)


def _qproj_kernel(cq_ref, gq_ref, w_ref, gqkn_ref, gkkn_ref, gqkr_ref, cos_ref, sin_ref, q_ref):
    h = _rms_rows(cq_ref[...], gq_ref[...]).astype(BF16)
    acc = jnp.dot(h, w_ref[...], preferred_element_type=F32)
    cos_t = cos_ref[...]
    sin_t = sin_ref[...]
    lane = lax.broadcasted_iota(jnp.int32, cos_t.shape, 1)
    low = lane < ROPE_DIM
    nope_w = MLA_HEADS * NOPE_DIM
    for p in range(MLA_HEADS // 2):
        r = acc[:, nope_w + LANES * p: nope_w + LANES * (p + 1)]
        r2 = r * r
        ssr0 = jnp.sum(jnp.where(low, r2, 0.0), axis=-1, keepdims=True)
        ssr1 = jnp.sum(jnp.where(low, 0.0, r2), axis=-1, keepdims=True)
        invs = []
        for u, ssr in ((0, ssr0), (1, ssr1)):
            hd = 2 * p + u
            nq = acc[:, NOPE_DIM * hd: NOPE_DIM * (hd + 1)]
            ss = jnp.sum(nq * nq, axis=-1, keepdims=True) + ssr
            inv = lax.rsqrt(ss / QK_DIM + EPS)
            invs.append(inv)
            qn = nq * inv * gqkn_ref[...] * gkkn_ref[...]
            q_ref[hd, :, 0:NOPE_DIM] = (qn * ATTN_SCALE).astype(BF16)
        rn = r * jnp.where(low, invs[0], invs[1]) * gqkr_ref[...]
        rr = (_rope128(rn, cos_t, sin_t) * ATTN_SCALE).astype(BF16)
        q_ref[2 * p, :, NOPE_DIM:QK_DIM] = rr[:, 0:ROPE_DIM]
        q_ref[2 * p + 1, :, NOPE_DIM:QK_DIM] = rr[:, ROPE_DIM:LANES]


def _qproj(z, g_q, w_uq_r, gqkn, gkkn, gqkr, cos_t, sin_t):
    m = z.shape[0]
    bm = _pick(m, 512, 16)
    n = w_uq_r.shape[1]
    return pl.pallas_call(
        _qproj_kernel,
        grid=(m // bm,),
        in_specs=[pl.BlockSpec((bm, Q_LORA), lambda i: (i, Z_CQ // Q_LORA)),
                  pl.BlockSpec((1, Q_LORA), lambda i: (0, 0)),
                  pl.BlockSpec((Q_LORA, n), lambda i: (0, 0)),
                  pl.BlockSpec((1, NOPE_DIM), lambda i: (0, 0)),
                  pl.BlockSpec((1, NOPE_DIM), lambda i: (0, 0)),
                  pl.BlockSpec((1, LANES), lambda i: (0, 0)),
                  pl.BlockSpec((bm, LANES), lambda i: (i, 0)),
                  pl.BlockSpec((bm, LANES), lambda i: (i, 0))],
        out_specs=pl.BlockSpec((MLA_HEADS, bm, QK_DIM), lambda i: (0, i, 0)),
        out_shape=jax.ShapeDtypeStruct((MLA_HEADS, m, QK_DIM), BF16),
        compiler_params=_params("parallel"),
        name="qproj",
    )(z, g_q, w_uq_r, gqkn, gkkn, gqkr, cos_t, sin_t)


def _kvproj_kernel(ckv_ref, kr_ref, gkv_ref, gkkr_ref, wuk_ref, wuv_ref, cos_ref, sin_ref,
                   rows_ref, ks_ref, k_ref, v_ref):
    c = _rms_rows(ckv_ref[...], gkv_ref[...])
    rows_ref[:, 0:KV_LORA] = c
    cb = c.astype(BF16)
    kn = jnp.dot(cb, wuk_ref[...], preferred_element_type=F32)
    v = jnp.dot(cb, wuv_ref[...], preferred_element_type=F32)
    kr = kr_ref[...]
    ss_kr = jnp.sum(kr * kr, axis=-1, keepdims=True)
    kpe = _rope128(kr * gkkr_ref[...], cos_ref[...], sin_ref[...])
    rows_ref[:, KV_LORA:KV_LORA + ROPE_DIM] = kpe[:, 0:ROPE_DIM]
    lane = lax.broadcasted_iota(jnp.int32, kr.shape, 1)
    ks_all = jnp.zeros(kr.shape, F32)
    for hd in range(MLA_HEADS):
        knh = kn[:, NOPE_DIM * hd: NOPE_DIM * (hd + 1)]
        ks = lax.rsqrt((jnp.sum(knh * knh, axis=-1, keepdims=True) + ss_kr) / QK_DIM + EPS)
        k_ref[hd, :, 0:NOPE_DIM] = (knh * ks).astype(BF16)
        k_ref[hd, :, NOPE_DIM:QK_DIM] = (kpe * ks).astype(BF16)[:, 0:ROPE_DIM]
        v_ref[hd] = v[:, V_DIM * hd: V_DIM * (hd + 1)].astype(BF16)
        ks_all = jnp.where(lane == hd, ks, ks_all)
    ks_ref[...] = ks_all[:, 0:MLA_HEADS]


def _kvproj(z, g_kv, gkkr, w_uk2, w_uv2, cos_t, sin_t):
    m = z.shape[0]
    bm = _pick(m, 512, 16)
    return pl.pallas_call(
        _kvproj_kernel,
        grid=(m // bm,),
        in_specs=[pl.BlockSpec((bm, KV_LORA), lambda i: (i, Z_CKV // KV_LORA)),
                  pl.BlockSpec((bm, LANES), lambda i: (i, Z_KR // LANES)),
                  pl.BlockSpec((1, KV_LORA), lambda i: (0, 0)),
                  pl.BlockSpec((1, LANES), lambda i: (0, 0)),
                  pl.BlockSpec((KV_LORA, MLA_HEADS * NOPE_DIM), lambda i: (0, 0)),
                  pl.BlockSpec((KV_LORA, MLA_HEADS * V_DIM), lambda i: (0, 0)),
                  pl.BlockSpec((bm, LANES), lambda i: (i, 0)),
                  pl.BlockSpec((bm, LANES), lambda i: (i, 0))],
        out_specs=[pl.BlockSpec((bm, KV_LORA + ROPE_DIM), lambda i: (i, 0)),
                   pl.BlockSpec((bm, MLA_HEADS), lambda i: (i, 0)),
                   pl.BlockSpec((MLA_HEADS, bm, QK_DIM), lambda i: (0, i, 0)),
                   pl.BlockSpec((MLA_HEADS, bm, V_DIM), lambda i: (0, i, 0))],
        out_shape=[jax.ShapeDtypeStruct((m, KV_LORA + ROPE_DIM), F32),
                   jax.ShapeDtypeStruct((m, MLA_HEADS), F32),
                   jax.ShapeDtypeStruct((MLA_HEADS, m, QK_DIM), BF16),
                   jax.ShapeDtypeStruct((MLA_HEADS, m, V_DIM), BF16)],
        compiler_params=_params("parallel"),
        name="kvproj",
    )(z, z, g_kv, gkkr, w_uk2, w_uv2, cos_t, sin_t)


def _flash_kernel(q_ref, k_ref, v_ref, o_ref, *, tq):
    qi = pl.program_id(2)
    q = q_ref[...]

    def step(kb, carry, masked):
        m, l, acc = carry
        r0 = pl.multiple_of(kb * tq, tq)
        k = k_ref[pl.ds(r0, tq), :]
        v = v_ref[pl.ds(r0, tq), :]
        s = lax.dot_general(q, k, (((1,), (1,)), ((), ())), preferred_element_type=F32)
        if masked:
            row = lax.broadcasted_iota(jnp.int32, s.shape, 0)
            col = lax.broadcasted_iota(jnp.int32, s.shape, 1)
            s = jnp.where(col <= row, s, NEG_BIG)
        m_new = jnp.maximum(m, jnp.max(s, axis=-1, keepdims=True))
        p = jnp.exp(s - m_new)
        corr = jnp.exp(m - m_new)
        l = l * corr + jnp.sum(p, axis=-1, keepdims=True)
        acc = acc * corr + jnp.dot(p.astype(BF16), v, preferred_element_type=F32)
        return m_new, l, acc

    init = (jnp.full((tq, 1), NEG_BIG, F32), jnp.zeros((tq, 1), F32), jnp.zeros((tq, V_DIM), F32))
    carry = lax.fori_loop(0, qi, lambda kb, c: step(kb, c, False), init)
    m, l, acc = step(qi, carry, True)
    o_ref[...] = (acc / l).astype(o_ref.dtype)


def _flash(q, k, v, batch, seq):
    tq = _pick(seq, 512, 16)
    nq = seq // tq
    mp = batch * seq
    return pl.pallas_call(
        functools.partial(_flash_kernel, tq=tq),
        grid=(batch, MLA_HEADS, nq),
        in_specs=[pl.BlockSpec((None, tq, QK_DIM), lambda b, h, i: (h, b * nq + i, 0)),
                  pl.BlockSpec((None, seq, QK_DIM), lambda b, h, i: (h, b, 0)),
                  pl.BlockSpec((None, seq, V_DIM), lambda b, h, i: (h, b, 0))],
        out_specs=pl.BlockSpec((tq, V_DIM), lambda b, h, i: (b * nq + i, h)),
        out_shape=jax.ShapeDtypeStruct((mp, MLA_HEADS * V_DIM), BF16),
        compiler_params=_params("parallel", "parallel", "arbitrary"),
        name="flash_prompt",
    )(q, k, v)


def _absorb_kernel(q_ref, w_ref, o_ref):
    q = q_ref[...]
    qa = jnp.dot(q[:, 0:NOPE_DIM], w_ref[...], preferred_element_type=F32)
    o_ref[:, 0:KV_LORA] = qa.astype(BF16)
    o_ref[:, KV_LORA:KV_LORA + ROPE_DIM] = q[:, NOPE_DIM:QK_DIM]


def _absorb(q, w_uk_t, mp, ms):
    assert mp % ms == 0
    return pl.pallas_call(
        _absorb_kernel,
        grid=(MLA_HEADS,),
        in_specs=[pl.BlockSpec((None, ms, QK_DIM), lambda h: (h, mp // ms, 0)),
                  pl.BlockSpec((None, NOPE_DIM, KV_LORA), lambda h: (h, 0, 0))],
        out_specs=pl.BlockSpec((None, ms, KV_LORA + ROPE_DIM), lambda h: (h, 0, 0)),
        out_shape=jax.ShapeDtypeStruct((MLA_HEADS, ms, KV_LORA + ROPE_DIM), BF16),
        compiler_params=_params("parallel"),
        name="absorb_q",
    )(q, w_uk_t)


def _split3(x):
    hi = x.astype(BF16)
    r1 = x - hi.astype(F32)
    mid = r1.astype(BF16)
    lo = (r1 - mid.astype(F32)).astype(BF16)
    return hi, mid, lo


def _paged_kernel(pt_ref, q_ref, e_ref, lat_hbm, ks_hbm, new_rows_ref, new_ks_ref, o_ref,
                  lat_buf, ks_buf, sems, m_scr, l_scr, acc_scr, tail_lat, tail_ks,
                  *, layer, pages_per_step, n_chunks, n_steps, dec_seq, page):
    g = pages_per_step
    b = pl.program_id(0)
    c = pl.program_id(1)
    step = b * n_chunks + c
    slot = lax.rem(step, 2)
    rows = q_ref.shape[0]
    nt_dims = (((1,), (1,)), ((), ()))

    def page_copies(bb, cc, sl):
        copies = []
        for k in range(g):
            pg = pt_ref[bb, cc * g + k]
            copies.append(pltpu.make_async_copy(lat_hbm.at[layer, pg], lat_buf.at[sl, k], sems.at[0, sl]))
            copies.append(pltpu.make_async_copy(ks_hbm.at[layer, pg], ks_buf.at[sl, k], sems.at[1, sl]))
        return copies

    @pl.when(step == 0)
    def _():
        for cp in page_copies(b, c, slot):
            cp.start()
        tail_lat[...] = jnp.zeros(tail_lat.shape, F32)
        tail_ks[...] = jnp.zeros(tail_ks.shape, F32)

    @pl.when(step + 1 < n_steps)
    def _():
        wrap = c + 1 == n_chunks
        for cp in page_copies(jnp.where(wrap, b + 1, b), jnp.where(wrap, 0, c + 1), 1 - slot):
            cp.start()

    for cp in page_copies(b, c, slot):
        cp.wait()

    @pl.when(c == 0)
    def _():
        m_scr[...] = jnp.full(m_scr.shape, NEG_BIG, F32)
        l_scr[...] = jnp.zeros(l_scr.shape, F32)
        acc_scr[...] = jnp.zeros(acc_scr.shape, F32)

    q = q_ref[...]
    e = e_ref[...]

    def select_heads(ks, dims):
        out = None
        for part in _split3(ks):
            t = lax.dot_general(e, part, dims, preferred_element_type=F32)
            out = t if out is None else out + t
        return out

    def softmax_update(s, pv):
        m = m_scr[...]
        m_new = jnp.maximum(m, jnp.max(s, axis=-1, keepdims=True))
        p = jnp.exp(s - m_new[:, 0:1])
        corr = jnp.exp(m - m_new)
        l_scr[...] = l_scr[...] * corr + jnp.sum(p, axis=-1, keepdims=True)
        acc_scr[...] = acc_scr[...] * corr[:, 0:1] + pv(p.astype(BF16))
        m_scr[...] = m_new

    lat_t = jnp.concatenate([lat_buf[slot, k].astype(BF16) for k in range(g)], axis=1)
    ks_t = jnp.concatenate([ks_buf[slot, k] for k in range(g)], axis=1)
    s = jnp.dot(q, lat_t, preferred_element_type=F32) * select_heads(ks_t, (((1,), (0,)), ((), ())))
    softmax_update(s, lambda pb: lax.dot_general(pb, lat_t[0:KV_LORA, :], nt_dims, preferred_element_type=F32))

    @pl.when(c == n_chunks - 1)
    def _():
        tail_lat[0:dec_seq, :] = new_rows_ref[...]
        tail_ks[0:dec_seq, :] = new_ks_ref[...]
        tl = tail_lat[...].astype(BF16)
        st = lax.dot_general(q, tl, nt_dims, preferred_element_type=F32) * select_heads(tail_ks[...], nt_dims)
        row = lax.broadcasted_iota(jnp.int32, (rows, page), 0)
        col = lax.broadcasted_iota(jnp.int32, (rows, page), 1)
        st = jnp.where(col <= (row % dec_seq), st, NEG_BIG)
        softmax_update(st, lambda pb: jnp.dot(pb, tl[:, 0:KV_LORA], preferred_element_type=F32))
        o_ref[...] = acc_scr[...] / l_scr[:, 0:1]


def _paged_attention(layer, page_table, q_s, e_sel, cache_latent, cache_kscale, new_rows, new_ks, pages_per_step):
    bs, rows, _ = q_s.shape
    n_pages = page_table.shape[1]
    page = cache_latent.shape[3]
    dec_seq = new_rows.shape[1]
    g = pages_per_step
    assert n_pages % g == 0
    nc = n_pages // g
    lat_w = KV_LORA + ROPE_DIM

    in_specs = [pl.BlockSpec((None, rows, lat_w), lambda b, c, pt: (b, 0, 0)),
                pl.BlockSpec((rows, MLA_HEADS), lambda b, c, pt: (0, 0)),
                pl.BlockSpec(memory_space=pl.ANY),
                pl.BlockSpec(memory_space=pl.ANY),
                pl.BlockSpec((None, dec_seq, lat_w), lambda b, c, pt: (b, 0, 0)),
                pl.BlockSpec((None, dec_seq, MLA_HEADS), lambda b, c, pt: (b, 0, 0))]
    grid_spec = pltpu.PrefetchScalarGridSpec(
        num_scalar_prefetch=1,
        grid=(bs, nc),
        in_specs=in_specs,
        out_specs=pl.BlockSpec((None, rows, KV_LORA), lambda b, c, pt: (b, 0, 0)),
        scratch_shapes=[pltpu.VMEM((2, g, lat_w, page), F32), pltpu.VMEM((2, g, MLA_HEADS, page), F32),
                        pltpu.SemaphoreType.DMA((2, 2)),
                        pltpu.VMEM((rows, LANES), F32), pltpu.VMEM((rows, LANES), F32),
                        pltpu.VMEM((rows, KV_LORA), F32),
                        pltpu.VMEM((page, lat_w), F32), pltpu.VMEM((page, MLA_HEADS), F32)])
    return pl.pallas_call(
        functools.partial(_paged_kernel, layer=layer, pages_per_step=g, n_chunks=nc, n_steps=bs * nc,
                          dec_seq=dec_seq, page=page),
        grid_spec=grid_spec,
        out_shape=jax.ShapeDtypeStruct((bs, rows, KV_LORA), F32),
        compiler_params=_params("arbitrary", "arbitrary"),
        name="paged_attention",
    )(page_table, q_s, e_sel, cache_latent, cache_kscale, new_rows, new_ks)


def _headmm_kernel(x_ref, w_ref, o_ref):
    o_ref[...] = jnp.dot(x_ref[...], w_ref[...], preferred_element_type=F32).astype(o_ref.dtype)


def _sample_vproj(o_lat_h, w_uv_h):
    h, ms, k = o_lat_h.shape
    return pl.pallas_call(
        _headmm_kernel,
        grid=(h,),
        in_specs=[pl.BlockSpec((None, ms, k), lambda i: (i, 0, 0)),
                  pl.BlockSpec((None, k, V_DIM), lambda i: (i, 0, 0))],
        out_specs=pl.BlockSpec((ms, V_DIM), lambda i: (0, i)),
        out_shape=jax.ShapeDtypeStruct((ms, h * V_DIM), BF16),
        compiler_params=_params("parallel"),
        name="sample_vproj",
    )(o_lat_h, w_uv_h)


def _ln_silu(y, g, b):
    yc = y - jnp.mean(y, axis=-1, keepdims=True)
    var = jnp.mean(yc * yc, axis=-1, keepdims=True)
    return _silu(yc * lax.rsqrt(var + EPS) * g + b)


CONV_HALO = 32
CONV_ROWS = 32


def _conv_prompt_kernel(a_ref, g_ref, w_ref, bdw_ref, gln_ref, bln_ref, yc_ref, st_ref, buf, shifted,
                        *, t_rows, nt):
    t = pl.program_id(1)

    @pl.when(t == 0)
    def _():
        buf[0:CONV_HALO, :] = jnp.zeros((CONV_HALO, CONV_CH), F32)

    @pl.when(t > 0)
    def _():
        buf[0:CONV_HALO, :] = buf[t_rows:t_rows + CONV_HALO, :]

    buf[CONV_HALO:CONV_HALO + t_rows, :] = a_ref[...] * _sigmoid(g_ref[...])
    off = CONV_HALO - (CONV_K - 1)
    span = t_rows + CONV_HALO - SUBLANES
    for r in range(1, SUBLANES):
        shifted[r - 1, 0:span, :] = buf[r:r + span, :]
    for blk in range(t_rows // CONV_ROWS):
        base = blk * CONV_ROWS
        acc = jnp.broadcast_to(bdw_ref[...], (CONV_ROWS, CONV_CH))
        for j in range(CONV_K):
            q, r = divmod(off + j, SUBLANES)
            lo = base + q * SUBLANES
            rows = buf[lo:lo + CONV_ROWS, :] if r == 0 else shifted[r - 1, lo:lo + CONV_ROWS, :]
            acc = acc + rows * w_ref[j:j + 1, :]
        yc_ref[base:base + CONV_ROWS, :] = _ln_silu(acc, gln_ref[...], bln_ref[...]).astype(BF16)

    @pl.when(t == nt - 1)
    def _():
        st_ref[...] = buf[t_rows + off:t_rows + CONV_HALO, :]


def _conv_prompt(z, w_dw, b_dw, g_ln, b_ln, batch, seq):
    t_rows = _pick(seq, 256, CONV_ROWS)
    nt = seq // t_rows
    mp = batch * seq
    row = lambda b, t: (0, 0)
    return pl.pallas_call(
        functools.partial(_conv_prompt_kernel, t_rows=t_rows, nt=nt),
        grid=(batch, nt),
        in_specs=[pl.BlockSpec((t_rows, CONV_CH), lambda b, t: (b * nt + t, Z_CONV_A // CONV_CH)),
                  pl.BlockSpec((t_rows, CONV_CH), lambda b, t: (b * nt + t, Z_CONV_G // CONV_CH)),
                  pl.BlockSpec((CONV_HALO, CONV_CH), row),
                  pl.BlockSpec((1, CONV_CH), row), pl.BlockSpec((1, CONV_CH), row), pl.BlockSpec((1, CONV_CH), row)],
        out_specs=[pl.BlockSpec((t_rows, CONV_CH), lambda b, t: (b * nt + t, 0)),
                   pl.BlockSpec((None, CONV_K - 1, CONV_CH), lambda b, t: (b, 0, 0))],
        out_shape=[jax.ShapeDtypeStruct((mp, CONV_CH), BF16),
                   jax.ShapeDtypeStruct((batch, CONV_K - 1, CONV_CH), F32)],
        scratch_shapes=[pltpu.VMEM((t_rows + CONV_HALO, CONV_CH), F32),
                        pltpu.VMEM((SUBLANES - 1, t_rows + CONV_HALO, CONV_CH), F32)],
        compiler_params=_params("parallel", "arbitrary"),
        name="conv_prompt",
    )(z, z, w_dw, b_dw, g_ln, b_ln)


def _conv_sample_kernel(a_ref, g_ref, st_ref, w_ref, bdw_ref, gln_ref, bln_ref, yc_ref, nst_ref, buf, ybuf,
                        *, n_seq, dec_seq):
    hist = CONV_K - 1
    a = a_ref[...] * _sigmoid(g_ref[...])
    w = w_ref[0:CONV_K, :]
    for s in range(n_seq):
        buf[s, 0:hist, :] = st_ref[s]
        buf[s, hist:hist + dec_seq, :] = a[s * dec_seq:(s + 1) * dec_seq, :]
    for s in range(n_seq):
        for t in range(dec_seq):
            y = jnp.sum(buf[s, t:t + CONV_K, :] * w, axis=0, keepdims=True) + bdw_ref[...]
            ybuf[s * dec_seq + t: s * dec_seq + t + 1, :] = y
        nst_ref[s] = buf[s, dec_seq:dec_seq + hist, :]
    yc_ref[...] = _ln_silu(ybuf[...], gln_ref[...], bln_ref[...]).astype(BF16)


def _conv_sample(layer, z, state_conv, w_dw, b_dw, g_ln, b_ln, mp, bs, dec_seq):
    n_seq = _pick(bs, 8, 1)
    rows = n_seq * dec_seq
    assert rows % 16 == 0 and mp % rows == 0
    hist = CONV_K - 1
    row = lambda i: (0, 0)
    return pl.pallas_call(
        functools.partial(_conv_sample_kernel, n_seq=n_seq, dec_seq=dec_seq),
        grid=(bs // n_seq,),
        in_specs=[pl.BlockSpec((rows, CONV_CH), lambda i: (mp // rows + i, Z_CONV_A // CONV_CH)),
                  pl.BlockSpec((rows, CONV_CH), lambda i: (mp // rows + i, Z_CONV_G // CONV_CH)),
                  pl.BlockSpec((None, n_seq, hist, CONV_CH), lambda i: (layer, i, 0, 0)),
                  pl.BlockSpec((CONV_HALO, CONV_CH), row),
                  pl.BlockSpec((1, CONV_CH), row), pl.BlockSpec((1, CONV_CH), row), pl.BlockSpec((1, CONV_CH), row)],
        out_specs=[pl.BlockSpec((rows, CONV_CH), lambda i: (i, 0)),
                   pl.BlockSpec((n_seq, hist, CONV_CH), lambda i: (i, 0, 0))],
        out_shape=[jax.ShapeDtypeStruct((bs * dec_seq, CONV_CH), BF16),
                   jax.ShapeDtypeStruct((bs, hist, CONV_CH), F32)],
        scratch_shapes=[pltpu.VMEM((n_seq, hist + SUBLANES + 2, CONV_CH), F32), pltpu.VMEM((rows, CONV_CH), F32)],
        compiler_params=_params("parallel"),
        name="conv_sample",
    )(z, z, state_conv, w_dw, b_dw, g_ln, b_ln)


def _hgrn_gates(hq, hf, hi, lb):
    log_lb = jnp.log(jnp.maximum(lb, LB_FLOOR))
    log_1m = jnp.log1p(-lb)
    log_sig = jnp.minimum(hf, 0.0) - jnp.log1p(jnp.exp(-jnp.abs(hf)))
    bb = log_1m + log_sig
    logf = jnp.maximum(log_lb, bb) + jnp.log1p(jnp.exp(-jnp.abs(log_lb - bb)))
    k = 1.0 - jnp.exp(logf)
    return _silu(hq), k, _silu(hi), logf


def _chunk_cumsum(bd, logf):
    out = None
    for part in _split3(logf):
        t = jnp.dot(bd, part, preferred_element_type=F32)
        out = t if out is None else out + t
    return out


def _intra_weights(q, k, cum, valid_fn, n_rows):
    lane = lax.broadcasted_iota(jnp.int32, (SUBLANES, LANES), 1)
    srow = lax.broadcasted_iota(jnp.int32, (SUBLANES, LANES), 0)
    blocks = [jnp.zeros((SUBLANES, LANES), F32) for _ in range(n_rows // SUBLANES)]
    for t in range(n_rows):
        qt = q[t:t + 1, :]
        ct = cum[t:t + 1, :]
        for rb in range(n_rows // SUBLANES):
            mask = valid_fn(srow + rb * SUBLANES, t, rb)
            if mask is None:
                continue
            sl = slice(rb * SUBLANES, (rb + 1) * SUBLANES)
            d = ct - cum[sl, :]
            if mask is not True:
                d = jnp.where(mask, d, NEG_BIG)
            w = jnp.sum(jnp.exp(d) * (qt * k[sl, :]), axis=-1, keepdims=True)
            blocks[rb] = jnp.where(lane == t, w, blocks[rb])
    return jnp.concatenate(blocks, axis=0)


def _chunk_attention(qc, kc, cc):
    n = qc.shape[0]
    nb = n // SUBLANES
    lane = lax.broadcasted_iota(jnp.int32, (SUBLANES, LANES), 1)
    srow = lax.broadcasted_iota(jnp.int32, (SUBLANES, LANES), 0)
    blocks = []
    for ib in range(nb):
        sl = slice(ib * SUBLANES, (ib + 1) * SUBLANES)
        cb, kb = cc[sl, :], kc[sl, :]
        blk = jnp.zeros((SUBLANES, LANES), F32)
        for tt in range(SUBLANES):
            t = ib * SUBLANES + tt
            d = jnp.where(srow <= tt, cc[t:t + 1, :] - cb, NEG_BIG)
            w = jnp.sum(jnp.exp(d) * (qc[t:t + 1, :] * kb), axis=-1, keepdims=True)
            blk = jnp.where(lane == t, w, blk)
        blocks.append(blk)
    att = jnp.concatenate(blocks, axis=0)
    if nb == 1:
        return att
    ks_parts, qs_parts = [], []
    for ib in range(1, nb):
        lo = ib * SUBLANES
        ref = cc[lo - 1:lo, :]
        kpart = kc[0:lo, :] * jnp.exp(ref - cc[0:lo, :])
        qpart = qc[lo:lo + SUBLANES, :] * jnp.exp(cc[lo:lo + SUBLANES, :] - ref)
        ks_parts.append(jnp.concatenate([kpart, jnp.zeros((n - lo, LANES), F32)], axis=0))
        qs_parts.append(jnp.concatenate([jnp.zeros((lo, LANES), F32), qpart,
                                         jnp.zeros((LANES - lo - SUBLANES, LANES), F32)], axis=0))
    ks = jnp.concatenate(ks_parts, axis=1).astype(BF16)
    qs = jnp.concatenate(qs_parts, axis=1).astype(BF16)
    return att + lax.dot_general(ks, qs, (((1,), (1,)), ((), ())), preferred_element_type=F32)


def _hgrn_prompt_kernel(hq_ref, hf_ref, hi_ref, hg_ref, lb_ref, go_ref, bd_ref, o_ref, s_ref, st_scr,
                        *, t_rows, nt, chunk):
    t = pl.program_id(2)
    tn_dims = (((0,), (0,)), ((), ()))

    @pl.when(t == 0)
    def _():
        st_scr[...] = jnp.zeros(st_scr.shape, F32)

    q, k, v, logf = _hgrn_gates(hq_ref[...], hf_ref[...], hi_ref[...], lb_ref[...])
    cum = _chunk_cumsum(bd_ref[...], logf)
    st = st_scr[...]
    outs = []
    for ci in range(t_rows // chunk):
        sl = slice(ci * chunk, (ci + 1) * chunk)
        qc, kc, cc = q[sl, :], k[sl, :], cum[sl, :]
        vb = v[sl, :].astype(BF16)
        att_t = _chunk_attention(qc, kc, cc)
        intra = lax.dot_general(att_t.astype(BF16), vb, tn_dims, preferred_element_type=F32)
        last = cc[chunk - 1:chunk, :]
        kv = lax.dot_general(vb, (kc * jnp.exp(last - cc)).astype(BF16), tn_dims, preferred_element_type=F32)
        inter = lax.dot_general((qc * jnp.exp(cc)).astype(BF16), st.astype(BF16),
                                (((1,), (1,)), ((), ())), preferred_element_type=F32)
        outs.append(inter + intra[0:chunk, :])
        st = st * jnp.exp(last) + kv
    st_scr[...] = st
    o = jnp.concatenate(outs, axis=0)
    o_ref[...] = (_rms_rows(o, go_ref[...]) * _sigmoid(hg_ref[...])).astype(BF16)

    @pl.when(t == nt - 1)
    def _():
        s_ref[...] = st_scr[...].T


def _block_tri(n, block):
    r = jnp.arange(n)
    return ((r[:, None] >= r[None, :]) & ((r[:, None] // block) == (r[None, :] // block))).astype(BF16)


def _hgrn_prompt(z, lb, g_o, batch, seq):
    chunk = HGRN_CHUNK if seq % HGRN_CHUNK == 0 else seq
    assert chunk % SUBLANES == 0 and chunk <= LANES
    t_rows = _pick(seq, 256, chunk)
    nt = seq // t_rows
    mp = batch * seq
    bd = _block_tri(t_rows, chunk)

    def zmap(off):
        return lambda b, h, t: (b * nt + t, off // HGRN_DK + h)

    return pl.pallas_call(
        functools.partial(_hgrn_prompt_kernel, t_rows=t_rows, nt=nt, chunk=chunk),
        grid=(batch, HGRN_HEADS, nt),
        in_specs=[pl.BlockSpec((t_rows, HGRN_DK), zmap(Z_HQ)),
                  pl.BlockSpec((t_rows, HGRN_DK), zmap(Z_HF)),
                  pl.BlockSpec((t_rows, HGRN_DK), zmap(Z_HI)),
                  pl.BlockSpec((t_rows, HGRN_DK), zmap(Z_HG)),
                  pl.BlockSpec((1, HGRN_DK), lambda b, h, t: (0, h)),
                  pl.BlockSpec((1, HGRN_DV), lambda b, h, t: (0, 0)),
                  pl.BlockSpec((t_rows, t_rows), lambda b, h, t: (0, 0))],
        out_specs=[pl.BlockSpec((t_rows, HGRN_DV), lambda b, h, t: (b * nt + t, h)),
                   pl.BlockSpec((None, None, HGRN_DK, HGRN_DV), lambda b, h, t: (b, h, 0, 0))],
        out_shape=[jax.ShapeDtypeStruct((mp, HGRN_HEADS * HGRN_DV), BF16),
                   jax.ShapeDtypeStruct((batch, HGRN_HEADS, HGRN_DK, HGRN_DV), F32)],
        scratch_shapes=[pltpu.VMEM((HGRN_DV, HGRN_DK), F32)],
        compiler_params=_params("parallel", "parallel", "arbitrary"),
        name="hgrn_prompt",
    )(z, z, z, z, lb, g_o, bd)


def _hgrn_sample_kernel(hq_ref, hf_ref, hi_ref, hg_ref, lb_ref, go_ref, bd_ref, s0_ref, o_ref, s_ref,
                        *, n_seq, dec_seq):
    rows = n_seq * dec_seq
    q, k, v, logf = _hgrn_gates(hq_ref[...], hf_ref[...], hi_ref[...], lb_ref[...])
    cum = _chunk_cumsum(bd_ref[...], logf)
    vb = v.astype(BF16)
    qe = (q * jnp.exp(cum)).astype(BF16)

    def valid(s_idx, t, rb):
        lo = (t // dec_seq) * dec_seq
        if lo // SUBLANES != rb:
            return None
        return jnp.logical_and(s_idx >= lo, s_idx <= t)

    att_t = _intra_weights(q, k, cum, valid, rows)
    intra = lax.dot_general(att_t.astype(BF16), vb, (((0,), (0,)), ((), ())),
                            preferred_element_type=F32)
    row = lax.broadcasted_iota(jnp.int32, (rows, HGRN_DK), 0)
    o = intra[0:rows, :]
    for s in range(n_seq):
        in_seq = (row // dec_seq) == s
        st = s0_ref[s].T
        inter = lax.dot_general(qe, st.astype(BF16), (((1,), (1,)), ((), ())), preferred_element_type=F32)
        o = o + jnp.where(in_seq, inter, 0.0)
        last = cum[(s + 1) * dec_seq - 1:(s + 1) * dec_seq, :]
        kt = (k * jnp.exp(jnp.where(in_seq, last - cum, NEG_BIG))).astype(BF16)
        st_new = st * jnp.exp(last) + lax.dot_general(vb, kt, (((0,), (0,)), ((), ())),
                                                      preferred_element_type=F32)
        s_ref[s] = st_new.T
    o_ref[...] = (_rms_rows(o, go_ref[...]) * _sigmoid(hg_ref[...])).astype(BF16)


def _hgrn_sample(layer, z, state_hgrn, lb, g_o, mp, bs, dec_seq):
    n_seq = _pick(bs, max(1, 16 // dec_seq), 1)
    rows = n_seq * dec_seq
    assert rows % 16 == 0 and mp % rows == 0 and SUBLANES % dec_seq == 0
    bd = _block_tri(rows, dec_seq)

    def zmap(off):
        return lambda i, h: (mp // rows + i, off // HGRN_DK + h)

    return pl.pallas_call(
        functools.partial(_hgrn_sample_kernel, n_seq=n_seq, dec_seq=dec_seq),
        grid=(bs // n_seq, HGRN_HEADS),
        in_specs=[pl.BlockSpec((rows, HGRN_DK), zmap(Z_HQ)),
                  pl.BlockSpec((rows, HGRN_DK), zmap(Z_HF)),
                  pl.BlockSpec((rows, HGRN_DK), zmap(Z_HI)),
                  pl.BlockSpec((rows, HGRN_DK), zmap(Z_HG)),
                  pl.BlockSpec((1, HGRN_DK), lambda i, h: (0, h)),
                  pl.BlockSpec((1, HGRN_DV), lambda i, h: (0, 0)),
                  pl.BlockSpec((rows, rows), lambda i, h: (0, 0)),
                  pl.BlockSpec((None, n_seq, None, HGRN_DK, HGRN_DV), lambda i, h: (layer, i, h, 0, 0))],
        out_specs=[pl.BlockSpec((rows, HGRN_DV), lambda i, h: (i, h)),
                   pl.BlockSpec((n_seq, None, HGRN_DK, HGRN_DV), lambda i, h: (i, h, 0, 0))],
        out_shape=[jax.ShapeDtypeStruct((bs * dec_seq, HGRN_HEADS * HGRN_DV), BF16),
                   jax.ShapeDtypeStruct((bs, HGRN_HEADS, HGRN_DK, HGRN_DV), F32)],
        compiler_params=_params("parallel", "parallel"),
        name="hgrn_sample",
    )(z, z, z, z, lb, g_o, bd, state_hgrn)


def _mix_kernel(om_ref, yc_ref, oh_ref, wm_ref, wc_ref, wh_ref, g0_ref, g1_ref, g2_ref, mix_ref):
    ym = jnp.dot(om_ref[...], wm_ref[...], preferred_element_type=F32)
    yc = jnp.dot(yc_ref[...], wc_ref[...], preferred_element_type=F32)
    yh = jnp.dot(oh_ref[...], wh_ref[...], preferred_element_type=F32)
    mix = _sigmoid(g0_ref[...]) * ym + _sigmoid(g1_ref[...]) * yc + _sigmoid(g2_ref[...]) * yh
    mix_ref[...] = mix.astype(BF16)


def _mix(o_mla, yc, o_h, w_o_mla, w_pw, w_o_hgrn, z):
    m = o_mla.shape[0]
    bm = _pick(m, 768, 16)
    bn = 512

    def gmap(off):
        return lambda i, j: (i, off // bn + j)

    return pl.pallas_call(
        _mix_kernel,
        grid=(m // bm, D_MODEL // bn),
        in_specs=[pl.BlockSpec((bm, MLA_HEADS * V_DIM), lambda i, j: (i, 0)),
                  pl.BlockSpec((bm, CONV_CH), lambda i, j: (i, 0)),
                  pl.BlockSpec((bm, HGRN_HEADS * HGRN_DV), lambda i, j: (i, 0)),
                  pl.BlockSpec((MLA_HEADS * V_DIM, bn), lambda i, j: (0, j)),
                  pl.BlockSpec((CONV_CH, bn), lambda i, j: (0, j)),
                  pl.BlockSpec((HGRN_HEADS * HGRN_DV, bn), lambda i, j: (0, j)),
                  pl.BlockSpec((bm, bn), gmap(Z_G0)),
                  pl.BlockSpec((bm, bn), gmap(Z_G1)),
                  pl.BlockSpec((bm, bn), gmap(Z_G2))],
        out_specs=pl.BlockSpec((bm, bn), lambda i, j: (i, j)),
        out_shape=jax.ShapeDtypeStruct((m, D_MODEL), BF16),
        compiler_params=_params("parallel", "arbitrary"),
        name="branch_mix",
    )(o_mla, yc, o_h, w_o_mla, w_pw, w_o_hgrn, z, z, z)


def _resmm_kernel(a_ref, w_ref, x_ref, o_ref):
    o_ref[...] = x_ref[...] + jnp.dot(a_ref[...], w_ref[...], preferred_element_type=F32)


def _residual_matmul(a, w, x, bm_target, bn):
    m, k = a.shape
    n = w.shape[1]
    bm = _pick(m, bm_target, 16)
    return pl.pallas_call(
        _resmm_kernel,
        grid=(m // bm, n // bn),
        in_specs=[pl.BlockSpec((bm, k), lambda i, j: (i, 0)),
                  pl.BlockSpec((k, bn), lambda i, j: (0, j)),
                  pl.BlockSpec((bm, bn), lambda i, j: (i, j))],
        out_specs=pl.BlockSpec((bm, bn), lambda i, j: (i, j)),
        out_shape=jax.ShapeDtypeStruct((m, n), F32),
        compiler_params=_params("parallel", "arbitrary"),
        name="residual_matmul",
    )(a, w, x)


def _ffn_up_kernel(x_ref, g_ref, wg_ref, wu_ref, o_ref, h_scr):
    @pl.when(pl.program_id(1) == 0)
    def _():
        h_scr[...] = _rms_rows(x_ref[...], g_ref[...]).astype(BF16)

    h = h_scr[...]
    gate = jnp.dot(h, wg_ref[...], preferred_element_type=F32)
    up = jnp.dot(h, wu_ref[...], preferred_element_type=F32)
    o_ref[...] = (_silu(gate) * up).astype(BF16)


def _ffn_up(x, g, w_gu):
    m, k = x.shape
    bm = _pick(m, 1536, 16)
    bn = 512
    nj = D_FF // bn
    return pl.pallas_call(
        _ffn_up_kernel,
        grid=(m // bm, nj),
        in_specs=[pl.BlockSpec((bm, k), lambda i, j: (i, 0)),
                  pl.BlockSpec((1, k), lambda i, j: (0, 0)),
                  pl.BlockSpec((k, bn), lambda i, j: (0, j)),
                  pl.BlockSpec((k, bn), lambda i, j: (0, nj + j))],
        out_specs=pl.BlockSpec((bm, bn), lambda i, j: (i, j)),
        out_shape=jax.ShapeDtypeStruct((m, D_FF), BF16),
        scratch_shapes=[pltpu.VMEM((bm, k), BF16)],
        compiler_params=_params("parallel", "arbitrary"),
        name="ffn_up",
    )(x, g, w_gu, w_gu)


def _ple_kernel(x_ref, g_ref, wg_ref, p_ref, wp_ref, xr_ref, o_ref, h_scr, p_scr):
    @pl.when(pl.program_id(1) == 0)
    def _():
        h_scr[...] = _rms_rows(x_ref[...], g_ref[...]).astype(BF16)
        p_scr[...] = p_ref[...].astype(BF16)

    gate = _sigmoid(jnp.dot(h_scr[...], wg_ref[...], preferred_element_type=F32))
    proj = jnp.dot(p_scr[...], wp_ref[...], preferred_element_type=F32)
    o_ref[...] = xr_ref[...] + proj * gate


def _ple(x, g, w_gate, p, w_proj):
    m, k = x.shape
    bm = _pick(m, 768, 16)
    bn = 512
    return pl.pallas_call(
        _ple_kernel,
        grid=(m // bm, D_MODEL // bn),
        in_specs=[pl.BlockSpec((bm, k), lambda i, j: (i, 0)),
                  pl.BlockSpec((1, k), lambda i, j: (0, 0)),
                  pl.BlockSpec((k, bn), lambda i, j: (0, j)),
                  pl.BlockSpec((bm, PLE_DIM), lambda i, j: (i, 0)),
                  pl.BlockSpec((PLE_DIM, bn), lambda i, j: (0, j)),
                  pl.BlockSpec((bm, bn), lambda i, j: (i, j))],
        out_specs=pl.BlockSpec((bm, bn), lambda i, j: (i, j)),
        out_shape=jax.ShapeDtypeStruct((m, D_MODEL), F32),
        scratch_shapes=[pltpu.VMEM((bm, k), BF16), pltpu.VMEM((bm, PLE_DIM), BF16)],
        compiler_params=_params("parallel", "arbitrary"),
        name="ple",
    )(x, g, w_gate, p, w_proj, x)


def _rope_tables(pos):
    half = ROPE_DIM // 2
    inv = ROPE_THETA ** (-jnp.arange(half, dtype=F32) / half)
    ang = pos.astype(F32)[:, None] * inv[None, :]
    cos, sin = jnp.cos(ang), jnp.sin(ang)
    reps = LANES // ROPE_DIM
    cos_t = jnp.tile(jnp.concatenate([cos, cos], axis=-1), (1, reps))
    sin_t = jnp.tile(jnp.concatenate([-sin, sin], axis=-1), (1, reps))
    return cos_t, sin_t


def _row(v):
    return v.reshape(1, -1).astype(F32)


def kernel(x_prompt, x_sample, cache_latent, cache_kscale, state_conv, state_hgrn, page_table, p_prompt, p_sample, g_mix, w_in, g_q, w_uq, g_kv, g_qk, g_kk, w_uk, w_uv, w_o_mla, w_dw, b_dw, g_cln, b_cln, w_pw, lb_logits, g_hgrn, w_o_hgrn, w_out, g_ffn, w_gu, w_down, g_ple, w_ple_gate, w_ple_proj):
    depth = w_in.shape[0]
    batch, seq, _ = x_prompt.shape
    bs, dec_seq, _ = x_sample.shape
    mp, ms = batch * seq, bs * dec_seq
    n_pages, page = page_table.shape[1], cache_latent.shape[2]
    past = n_pages * page

    lb_p = jax.nn.softmax(lb_logits.astype(F32), axis=0)
    lb_all = jnp.cumsum(lb_p, axis=0) - lb_p[0:1]

    cos_p, sin_p = _rope_tables(jnp.arange(seq, dtype=jnp.int32))
    cos_s, sin_s = _rope_tables(past + jnp.arange(dec_seq, dtype=jnp.int32))
    cos_t = jnp.concatenate([jnp.tile(cos_p, (batch, 1)), jnp.tile(cos_s, (bs, 1))], axis=0)
    sin_t = jnp.concatenate([jnp.tile(sin_p, (batch, 1)), jnp.tile(sin_s, (bs, 1))], axis=0)

    rows_q = MLA_HEADS * dec_seq
    e_sel = (jnp.arange(rows_q)[:, None] // dec_seq == jnp.arange(MLA_HEADS)[None, :]).astype(BF16)
    pages_per_step = _pick(n_pages, 32, 1)
    cache_latent_t = jnp.swapaxes(cache_latent, 2, 3)
    cache_kscale_t = jnp.swapaxes(cache_kscale, 2, 3)

    x = jnp.concatenate([x_prompt.reshape(mp, D_MODEL), x_sample.reshape(ms, D_MODEL)], axis=0)
    lat_p, ks_p, conv_p, hg_p, lat_s, ks_s, conv_s, hg_s = ([] for _ in range(8))
    for i in range(depth):
        wi = w_in[i]
        w_in_r = jnp.concatenate(
            [wi[:, 0:Q_LORA + KV_LORA], wi[:, Q_LORA + KV_LORA + ROPE_DIM:], wi[:, Q_LORA + KV_LORA:Q_LORA + KV_LORA + ROPE_DIM],
             jnp.zeros((D_MODEL, Z_COLS - wi.shape[1]), wi.dtype)], axis=1).astype(BF16)
        wq = w_uq[i].reshape(Q_LORA, MLA_HEADS, QK_DIM)
        w_uq_r = jnp.concatenate([wq[:, :, 0:NOPE_DIM].reshape(Q_LORA, -1), wq[:, :, NOPE_DIM:].reshape(Q_LORA, -1)],
                                 axis=1).astype(BF16)
        w_uk2 = w_uk[i].reshape(KV_LORA, -1).astype(BF16)
        w_uv2 = w_uv[i].reshape(KV_LORA, -1).astype(BF16)
        w_uk_t = jnp.transpose(w_uk[i], (1, 2, 0)).astype(BF16)
        w_uv_h = jnp.transpose(w_uv[i], (1, 0, 2)).astype(BF16)
        gqkn = _row(g_qk[i, 0:NOPE_DIM])
        gkkn = _row(g_kk[i, 0:NOPE_DIM])
        gqkr = _row(jnp.tile(g_qk[i, NOPE_DIM:], LANES // ROPE_DIM))
        gkkr = _row(jnp.concatenate([g_kk[i, NOPE_DIM:], jnp.zeros((LANES - ROPE_DIM,), F32)]))
        w_dw_p = jnp.concatenate([w_dw[i], jnp.zeros((CONV_HALO - CONV_K, CONV_CH), F32)], axis=0)
        lb = _row(lb_all[i])

        z = _inproj(x, _row(g_mix[i]), w_in_r)
        q_all = _qproj(z, _row(g_q[i]), w_uq_r, gqkn, gkkn, gqkr, cos_t, sin_t)
        rows, kscale, k_all, v_all = _kvproj(z, _row(g_kv[i]), gkkr, w_uk2, w_uv2, cos_t, sin_t)

        o_mla_p = _flash(q_all, k_all, v_all, batch, seq)
        q_s = _absorb(q_all, w_uk_t, mp, ms)
        q_s = q_s.reshape(MLA_HEADS, bs, dec_seq, -1).transpose(1, 0, 2, 3).reshape(bs, rows_q, -1)
        new_rows = rows[mp:].reshape(bs, dec_seq, -1)
        new_ks = kscale[mp:].reshape(bs, dec_seq, MLA_HEADS)
        o_lat = _paged_attention(i, page_table, q_s, e_sel, cache_latent_t, cache_kscale_t, new_rows, new_ks,
                                 pages_per_step)
        o_lat_h = o_lat.reshape(bs, MLA_HEADS, dec_seq, KV_LORA).transpose(1, 0, 2, 3)
        o_lat_h = o_lat_h.reshape(MLA_HEADS, ms, KV_LORA).astype(BF16)
        o_mla_s = _sample_vproj(o_lat_h, w_uv_h)
        o_mla = jnp.concatenate([o_mla_p, o_mla_s], axis=0)

        yc_p, conv_new_p = _conv_prompt(z, w_dw_p, _row(b_dw[i]), _row(g_cln[i]), _row(b_cln[i]), batch, seq)
        yc_s, conv_new_s = _conv_sample(i, z, state_conv, w_dw_p, _row(b_dw[i]), _row(g_cln[i]), _row(b_cln[i]),
                                        mp, bs, dec_seq)
        yc = jnp.concatenate([yc_p, yc_s], axis=0)

        oh_p, s_new_p = _hgrn_prompt(z, lb, _row(g_hgrn[i]), batch, seq)
        oh_s, s_new_s = _hgrn_sample(i, z, state_hgrn, lb, _row(g_hgrn[i]), mp, bs, dec_seq)
        o_h = jnp.concatenate([oh_p, oh_s], axis=0)

        mix = _mix(o_mla, yc, o_h, w_o_mla[i].astype(BF16), w_pw[i].astype(BF16), w_o_hgrn[i].astype(BF16), z)
        x = _residual_matmul(mix, w_out[i].astype(BF16), x, 1536, 512)

        act = _ffn_up(x, _row(g_ffn[i]), w_gu[i].astype(BF16))
        x = _residual_matmul(act, w_down[i].astype(BF16), x, 768, 512)
        p_emb = jnp.concatenate([p_prompt[i].reshape(mp, PLE_DIM), p_sample[i].reshape(ms, PLE_DIM)], axis=0)
        x = _ple(x, _row(g_ple[i]), w_ple_gate[i].astype(BF16), p_emb, w_ple_proj[i].astype(BF16))

        lat_p.append(rows[:mp].reshape(batch, seq, -1))
        ks_p.append(kscale[:mp].reshape(batch, seq, MLA_HEADS))
        conv_p.append(conv_new_p)
        hg_p.append(s_new_p)
        lat_s.append(new_rows)
        ks_s.append(new_ks)
        conv_s.append(conv_new_s)
        hg_s.append(s_new_s)

    return (x[:mp].reshape(batch, seq, D_MODEL), x[mp:].reshape(bs, dec_seq, D_MODEL),
            jnp.stack(lat_p), jnp.stack(ks_p), jnp.stack(conv_p), jnp.stack(hg_p),
            jnp.stack(lat_s), jnp.stack(ks_s), jnp.stack(conv_s), jnp.stack(hg_s))
```

```python
import functools

import jax
import jax.numpy as jnp
from jax import lax
from jax.experimental import pallas as pl
from jax.experimental.pallas import tpu as pltpu

F32 = jnp.float32
BF16 = jnp.bfloat16

D_MODEL = 2048
MLA_HEADS = 16
Q_LORA = 512
KV_LORA = 512
NOPE_DIM = 128
ROPE_DIM = 64
QK_DIM = NOPE_DIM + ROPE_DIM
V_DIM = 128
ROPE_THETA = 10000.0
ATTN_SCALE = QK_DIM ** -0.5
CONV_CH = 1024
CONV_K = 31
HGRN_HEADS = 8
HGRN_DK = 128
HGRN_DV = 128
HGRN_KW = HGRN_HEADS * HGRN_DK
HGRN_CHUNK = 32
LB_FLOOR = 1e-30
D_FF = ((8 * D_MODEL // 3 + 255) // 256) * 256
PLE_DIM = 256
EPS = 1e-6
NEG_BIG = -1e30

LANES = 128
SUBLANES = 8
VMEM_LIMIT_BYTES = 56 * 1024 * 1024

Z_CQ = 0
Z_CKV = 512
Z_CONV_A = 1024
Z_CONV_G = 2048
Z_HQ = 3072
Z_HF = 4096
Z_HI = 5120
Z_HG = 6144
Z_G0 = 7168
Z_G1 = 9216
Z_G2 = 11264
Z_KR = 13312
Z_COLS = 13824


def _pick(total, target, mult=SUBLANES):
    best = None
    for d in range(mult, min(total, target) + 1, mult):
        if total % d == 0:
            best = d
    assert best is not None, (total, target, mult)
    return best


def _params(*sem):
    return pltpu.CompilerParams(dimension_semantics=sem, vmem_limit_bytes=VMEM_LIMIT_BYTES)


def _rms_rows(x, g):
    ms = jnp.mean(x * x, axis=-1, keepdims=True)
    return x * lax.rsqrt(ms + EPS) * g


def _sigmoid(x):
    return 1.0 / (1.0 + jnp.exp(-x))


def _silu(x):
    return x * _sigmoid(x)


def _rope128(x, cos_t, sin_t):
    lane = lax.broadcasted_iota(jnp.int32, x.shape, 1)
    first_half = (lane & (ROPE_DIM // 2)) == 0
    partner = jnp.where(first_half, pltpu.roll(x, LANES - ROPE_DIM // 2, 1), pltpu.roll(x, ROPE_DIM // 2, 1))
    return x * cos_t + partner * sin_t


def _inproj_kernel(x_ref, g_ref, w_ref, z_ref, h_scr):
    @pl.when(pl.program_id(1) == 0)
    def _():
        h_scr[...] = _rms_rows(x_ref[...], g_ref[...]).astype(BF16)

    z_ref[...] = jnp.dot(h_scr[...], w_ref[...], preferred_element_type=F32)


def _inproj(x, g, w):
    m, k = x.shape
    n = w.shape[1]
    bm = _pick(m, 1536)
    bn = _pick(n, 768, LANES)
    return pl.pallas_call(
        _inproj_kernel,
        grid=(m // bm, n // bn),
        in_specs=[pl.BlockSpec((bm, k), lambda i, j: (i, 0)),
                  pl.BlockSpec((1, k), lambda i, j: (0, 0)),
                  pl.BlockSpec((k, bn), lambda i, j: (0, j))],
        out_specs=pl.BlockSpec((bm, bn), lambda i, j: (i, j)),
        out_shape=jax.ShapeDtypeStruct((m, n), F32),
        scratch_shapes=[pltpu.VMEM((bm, k), BF16)],
        compiler_params=_params("parallel", "arbitrary"),
        name="inproj",
    )(x, g, w)


def _qproj_kernel(cq_ref, gq_ref, w_ref, gqkn_ref, gkkn_ref, gqkr_ref, cos_ref, sin_ref, q_ref):
    h = _rms_rows(cq_ref[...], gq_ref[...]).astype(BF16)
    acc = jnp.dot(h, w_ref[...], preferred_element_type=F32)
    cos_t = cos_ref[...]
    sin_t = sin_ref[...]
    lane = lax.broadcasted_iota(jnp.int32, cos_t.shape, 1)
    low = lane < ROPE_DIM
    nope_w = MLA_HEADS * NOPE_DIM
    for p in range(MLA_HEADS // 2):
        r = acc[:, nope_w + LANES * p: nope_w + LANES * (p + 1)]
        r2 = r * r
        ssr0 = jnp.sum(jnp.where(low, r2, 0.0), axis=-1, keepdims=True)
        ssr1 = jnp.sum(jnp.where(low, 0.0, r2), axis=-1, keepdims=True)
        invs = []
        for u, ssr in ((0, ssr0), (1, ssr1)):
            hd = 2 * p + u
            nq = acc[:, NOPE_DIM * hd: NOPE_DIM * (hd + 1)]
            ss = jnp.sum(nq * nq, axis=-1, keepdims=True) + ssr
            inv = lax.rsqrt(ss / QK_DIM + EPS)
            invs.append(inv)
            qn = nq * inv * gqkn_ref[...] * gkkn_ref[...]
            q_ref[hd, :, 0:NOPE_DIM] = (qn * ATTN_SCALE).astype(BF16)
        rn = r * jnp.where(low, invs[0], invs[1]) * gqkr_ref[...]
        rr = (_rope128(rn, cos_t, sin_t) * ATTN_SCALE).astype(BF16)
        q_ref[2 * p, :, NOPE_DIM:QK_DIM] = rr[:, 0:ROPE_DIM]
        q_ref[2 * p + 1, :, NOPE_DIM:QK_DIM] = rr[:, ROPE_DIM:LANES]


def _qproj(z, g_q, w_uq_r, gqkn, gkkn, gqkr, cos_t, sin_t):
    m = z.shape[0]
    bm = _pick(m, 512, 16)
    n = w_uq_r.shape[1]
    return pl.pallas_call(
        _qproj_kernel,
        grid=(m // bm,),
        in_specs=[pl.BlockSpec((bm, Q_LORA), lambda i: (i, Z_CQ // Q_LORA)),
                  pl.BlockSpec((1, Q_LORA), lambda i: (0, 0)),
                  pl.BlockSpec((Q_LORA, n), lambda i: (0, 0)),
                  pl.BlockSpec((1, NOPE_DIM), lambda i: (0, 0)),
                  pl.BlockSpec((1, NOPE_DIM), lambda i: (0, 0)),
                  pl.BlockSpec((1, LANES), lambda i: (0, 0)),
                  pl.BlockSpec((bm, LANES), lambda i: (i, 0)),
                  pl.BlockSpec((bm, LANES), lambda i: (i, 0))],
        out_specs=pl.BlockSpec((MLA_HEADS, bm, QK_DIM), lambda i: (0, i, 0)),
        out_shape=jax.ShapeDtypeStruct((MLA_HEADS, m, QK_DIM), BF16),
        compiler_params=_params("parallel"),
        name="qproj",
    )(z, g_q, w_uq_r, gqkn, gkkn, gqkr, cos_t, sin_t)


def _kvproj_kernel(ckv_ref, kr_ref, gkv_ref, gkkr_ref, wuk_ref, wuv_ref, cos_ref, sin_ref,
                   rows_ref, ks_ref, k_ref, v_ref):
    c = _rms_rows(ckv_ref[...], gkv_ref[...])
    rows_ref[:, 0:KV_LORA] = c
    cb = c.astype(BF16)
    kn = jnp.dot(cb, wuk_ref[...], preferred_element_type=F32)
    v = jnp.dot(cb, wuv_ref[...], preferred_element_type=F32)
    kr = kr_ref[...]
    ss_kr = jnp.sum(kr * kr, axis=-1, keepdims=True)
    kpe = _rope128(kr * gkkr_ref[...], cos_ref[...], sin_ref[...])
    rows_ref[:, KV_LORA:KV_LORA + ROPE_DIM] = kpe[:, 0:ROPE_DIM]
    lane = lax.broadcasted_iota(jnp.int32, kr.shape, 1)
    ks_all = jnp.zeros(kr.shape, F32)
    for hd in range(MLA_HEADS):
        knh = kn[:, NOPE_DIM * hd: NOPE_DIM * (hd + 1)]
        ks = lax.rsqrt((jnp.sum(knh * knh, axis=-1, keepdims=True) + ss_kr) / QK_DIM + EPS)
        k_ref[hd, :, 0:NOPE_DIM] = (knh * ks).astype(BF16)
        k_ref[hd, :, NOPE_DIM:QK_DIM] = (kpe * ks).astype(BF16)[:, 0:ROPE_DIM]
        v_ref[hd] = v[:, V_DIM * hd: V_DIM * (hd + 1)].astype(BF16)
        ks_all = jnp.where(lane == hd, ks, ks_all)
    ks_ref[...] = ks_all[:, 0:MLA_HEADS]


def _kvproj(z, g_kv, gkkr, w_uk2, w_uv2, cos_t, sin_t):
    m = z.shape[0]
    bm = _pick(m, 512, 16)
    return pl.pallas_call(
        _kvproj_kernel,
        grid=(m // bm,),
        in_specs=[pl.BlockSpec((bm, KV_LORA), lambda i: (i, Z_CKV // KV_LORA)),
                  pl.BlockSpec((bm, LANES), lambda i: (i, Z_KR // LANES)),
                  pl.BlockSpec((1, KV_LORA), lambda i: (0, 0)),
                  pl.BlockSpec((1, LANES), lambda i: (0, 0)),
                  pl.BlockSpec((KV_LORA, MLA_HEADS * NOPE_DIM), lambda i: (0, 0)),
                  pl.BlockSpec((KV_LORA, MLA_HEADS * V_DIM), lambda i: (0, 0)),
                  pl.BlockSpec((bm, LANES), lambda i: (i, 0)),
                  pl.BlockSpec((bm, LANES), lambda i: (i, 0))],
        out_specs=[pl.BlockSpec((bm, KV_LORA + ROPE_DIM), lambda i: (i, 0)),
                   pl.BlockSpec((bm, MLA_HEADS), lambda i: (i, 0)),
                   pl.BlockSpec((MLA_HEADS, bm, QK_DIM), lambda i: (0, i, 0)),
                   pl.BlockSpec((MLA_HEADS, bm, V_DIM), lambda i: (0, i, 0))],
        out_shape=[jax.ShapeDtypeStruct((m, KV_LORA + ROPE_DIM), F32),
                   jax.ShapeDtypeStruct((m, MLA_HEADS), F32),
                   jax.ShapeDtypeStruct((MLA_HEADS, m, QK_DIM), BF16),
                   jax.ShapeDtypeStruct((MLA_HEADS, m, V_DIM), BF16)],
        compiler_params=_params("parallel"),
        name="kvproj",
    )(z, z, g_kv, gkkr, w_uk2, w_uv2, cos_t, sin_t)


def _flash_kernel(q_ref, k_ref, v_ref, o_ref, *, tq):
    qi = pl.program_id(2)
    q = q_ref[...]

    def step(kb, carry, masked, width=1):
        m, l, acc = carry
        r0 = pl.multiple_of(kb * tq, tq)
        k = k_ref[pl.ds(r0, width * tq), :]
        v = v_ref[pl.ds(r0, width * tq), :]
        s = lax.dot_general(q, k, (((1,), (1,)), ((), ())), preferred_element_type=F32)
        if masked:
            row = lax.broadcasted_iota(jnp.int32, s.shape, 0)
            col = lax.broadcasted_iota(jnp.int32, s.shape, 1)
            s = jnp.where(col <= row, s, NEG_BIG)
        m_new = jnp.maximum(m, jnp.max(s, axis=-1, keepdims=True))
        p = jnp.exp(s - m_new)
        corr = jnp.exp(m - m_new)
        l = l * corr + jnp.sum(p, axis=-1, keepdims=True)
        acc = acc * corr + jnp.dot(p.astype(BF16), v, preferred_element_type=F32)
        return m_new, l, acc

    init = (jnp.full((tq, 1), NEG_BIG, F32), jnp.zeros((tq, 1), F32), jnp.zeros((tq, V_DIM), F32))
    carry = lax.fori_loop(0, qi // 2, lambda pb, c: step(2 * pb, c, False, width=2), init)
    carry = lax.cond(qi % 2 == 1, lambda c: step(qi - 1, c, False), lambda c: c, carry)
    m, l, acc = step(qi, carry, True)
    o_ref[...] = (acc / l).astype(o_ref.dtype)


def _flash(q, k, v, batch, seq):
    tq = _pick(seq, 512, 16)
    nq = seq // tq
    mp = batch * seq
    return pl.pallas_call(
        functools.partial(_flash_kernel, tq=tq),
        grid=(batch, MLA_HEADS, nq),
        in_specs=[pl.BlockSpec((None, tq, QK_DIM), lambda b, h, i: (h, b * nq + i, 0)),
                  pl.BlockSpec((None, seq, QK_DIM), lambda b, h, i: (h, b, 0)),
                  pl.BlockSpec((None, seq, V_DIM), lambda b, h, i: (h, b, 0))],
        out_specs=pl.BlockSpec((tq, V_DIM), lambda b, h, i: (b * nq + i, h)),
        out_shape=jax.ShapeDtypeStruct((mp, MLA_HEADS * V_DIM), BF16),
        compiler_params=_params("parallel", "parallel", "arbitrary"),
        name="flash_prompt",
    )(q, k, v)


def _absorb_kernel(q_ref, w_ref, o_ref):
    q = q_ref[...]
    qa = jnp.dot(q[:, 0:NOPE_DIM], w_ref[...], preferred_element_type=F32)
    o_ref[:, 0:KV_LORA] = qa.astype(BF16)
    o_ref[:, KV_LORA:KV_LORA + ROPE_DIM] = q[:, NOPE_DIM:QK_DIM]


def _absorb(q, w_uk_t, mp, ms):
    assert mp % ms == 0
    return pl.pallas_call(
        _absorb_kernel,
        grid=(MLA_HEADS,),
        in_specs=[pl.BlockSpec((None, ms, QK_DIM), lambda h: (h, mp // ms, 0)),
                  pl.BlockSpec((None, NOPE_DIM, KV_LORA), lambda h: (h, 0, 0))],
        out_specs=pl.BlockSpec((None, ms, KV_LORA + ROPE_DIM), lambda h: (h, 0, 0)),
        out_shape=jax.ShapeDtypeStruct((MLA_HEADS, ms, KV_LORA + ROPE_DIM), BF16),
        compiler_params=_params("parallel"),
        name="absorb_q",
    )(q, w_uk_t)


def _split3(x):
    hi = x.astype(BF16)
    r1 = x - hi.astype(F32)
    mid = r1.astype(BF16)
    lo = (r1 - mid.astype(F32)).astype(BF16)
    return hi, mid, lo


def _paged_kernel(pt_ref, q_ref, e_ref, lat_hbm, ks_hbm, new_rows_ref, new_ks_ref, o_ref,
                  lat_buf, ks_buf, sems, m_scr, l_scr, acc_scr, tail_lat, tail_ks,
                  *, layer, pages_per_step, n_chunks, n_steps, dec_seq, page):
    g = pages_per_step
    b = pl.program_id(0)
    c = pl.program_id(1)
    step = b * n_chunks + c
    slot = lax.rem(step, 2)
    rows = q_ref.shape[0]
    nt_dims = (((1,), (1,)), ((), ()))

    def page_copies(bb, cc, sl):
        copies = []
        for k in range(g):
            pg = pt_ref[bb, cc * g + k]
            copies.append(pltpu.make_async_copy(lat_hbm.at[layer, pg], lat_buf.at[sl, k], sems.at[0, sl]))
            copies.append(pltpu.make_async_copy(ks_hbm.at[layer, pg], ks_buf.at[sl, k], sems.at[1, sl]))
        return copies

    @pl.when(step == 0)
    def _():
        for n, cp in enumerate(page_copies(b, c, slot)):
            cp.start(priority=(n // 2) % 2)
        tail_lat[...] = jnp.zeros(tail_lat.shape, F32)
        tail_ks[...] = jnp.zeros(tail_ks.shape, F32)

    @pl.when(step + 1 < n_steps)
    def _():
        wrap = c + 1 == n_chunks
        for n, cp in enumerate(page_copies(jnp.where(wrap, b + 1, b), jnp.where(wrap, 0, c + 1), 1 - slot)):
            cp.start(priority=(n // 2) % 2)

    for cp in page_copies(b, c, slot):
        cp.wait()

    @pl.when(c == 0)
    def _():
        m_scr[...] = jnp.full(m_scr.shape, NEG_BIG, F32)
        l_scr[...] = jnp.zeros(l_scr.shape, F32)
        acc_scr[...] = jnp.zeros(acc_scr.shape, F32)

    q = q_ref[...]
    e = e_ref[...]

    def select_heads(ks, dims):
        out = None
        for part in _split3(ks):
            t = lax.dot_general(e, part, dims, preferred_element_type=F32)
            out = t if out is None else out + t
        return out

    def softmax_update(s, pv):
        m = m_scr[...]
        m_new = jnp.maximum(m, jnp.max(s, axis=-1, keepdims=True))
        p = jnp.exp(s - m_new[:, 0:1])
        corr = jnp.exp(m - m_new)
        l_scr[...] = l_scr[...] * corr + jnp.sum(p, axis=-1, keepdims=True)
        acc_scr[...] = acc_scr[...] * corr[:, 0:1] + pv(p.astype(BF16))
        m_scr[...] = m_new

    lat_t = jnp.concatenate([lat_buf[slot, k].astype(BF16) for k in range(g)], axis=1)
    ks_t = jnp.concatenate([ks_buf[slot, k] for k in range(g)], axis=1)
    s = jnp.dot(q, lat_t, preferred_element_type=F32) * select_heads(ks_t, (((1,), (0,)), ((), ())))
    softmax_update(s, lambda pb: lax.dot_general(pb, lat_t[0:KV_LORA, :], nt_dims, preferred_element_type=F32))

    @pl.when(c == n_chunks - 1)
    def _():
        tail_lat[0:dec_seq, :] = new_rows_ref[...]
        tail_ks[0:dec_seq, :] = new_ks_ref[...]
        tl = tail_lat[...].astype(BF16)
        st = lax.dot_general(q, tl, nt_dims, preferred_element_type=F32) * select_heads(tail_ks[...], nt_dims)
        row = lax.broadcasted_iota(jnp.int32, (rows, page), 0)
        col = lax.broadcasted_iota(jnp.int32, (rows, page), 1)
        st = jnp.where(col <= (row % dec_seq), st, NEG_BIG)
        softmax_update(st, lambda pb: jnp.dot(pb, tl[:, 0:KV_LORA], preferred_element_type=F32))
        o_ref[...] = acc_scr[...] / l_scr[:, 0:1]


def _paged_attention(layer, page_table, q_s, e_sel, cache_latent, cache_kscale, new_rows, new_ks, pages_per_step):
    bs, rows, _ = q_s.shape
    n_pages = page_table.shape[1]
    page = cache_latent.shape[3]
    dec_seq = new_rows.shape[1]
    g = pages_per_step
    assert n_pages % g == 0
    nc = n_pages // g
    lat_w = KV_LORA + ROPE_DIM

    in_specs = [pl.BlockSpec((None, rows, lat_w), lambda b, c, pt: (b, 0, 0)),
                pl.BlockSpec((rows, MLA_HEADS), lambda b, c, pt: (0, 0)),
                pl.BlockSpec(memory_space=pl.ANY),
                pl.BlockSpec(memory_space=pl.ANY),
                pl.BlockSpec((None, dec_seq, lat_w), lambda b, c, pt: (b, 0, 0)),
                pl.BlockSpec((None, dec_seq, MLA_HEADS), lambda b, c, pt: (b, 0, 0))]
    grid_spec = pltpu.PrefetchScalarGridSpec(
        num_scalar_prefetch=1,
        grid=(bs, nc),
        in_specs=in_specs,
        out_specs=pl.BlockSpec((None, rows, KV_LORA), lambda b, c, pt: (b, 0, 0)),
        scratch_shapes=[pltpu.VMEM((2, g, lat_w, page), F32), pltpu.VMEM((2, g, MLA_HEADS, page), F32),
                        pltpu.SemaphoreType.DMA((2, 2)),
                        pltpu.VMEM((rows, LANES), F32), pltpu.VMEM((rows, LANES), F32),
                        pltpu.VMEM((rows, KV_LORA), F32),
                        pltpu.VMEM((page, lat_w), F32), pltpu.VMEM((page, MLA_HEADS), F32)])
    return pl.pallas_call(
        functools.partial(_paged_kernel, layer=layer, pages_per_step=g, n_chunks=nc, n_steps=bs * nc,
                          dec_seq=dec_seq, page=page),
        grid_spec=grid_spec,
        out_shape=jax.ShapeDtypeStruct((bs, rows, KV_LORA), F32),
        compiler_params=_params("arbitrary", "arbitrary"),
        name="paged_attention",
    )(page_table, q_s, e_sel, cache_latent, cache_kscale, new_rows, new_ks)


def _headmm_kernel(x_ref, w_ref, o_ref):
    o_ref[...] = jnp.dot(x_ref[...], w_ref[...], preferred_element_type=F32).astype(o_ref.dtype)


def _sample_vproj(o_lat_h, w_uv_h):
    h, ms, k = o_lat_h.shape
    return pl.pallas_call(
        _headmm_kernel,
        grid=(h,),
        in_specs=[pl.BlockSpec((None, ms, k), lambda i: (i, 0, 0)),
                  pl.BlockSpec((None, k, V_DIM), lambda i: (i, 0, 0))],
        out_specs=pl.BlockSpec((ms, V_DIM), lambda i: (0, i)),
        out_shape=jax.ShapeDtypeStruct((ms, h * V_DIM), BF16),
        compiler_params=_params("parallel"),
        name="sample_vproj",
    )(o_lat_h, w_uv_h)


def _ln_silu(y, g, b):
    yc = y - jnp.mean(y, axis=-1, keepdims=True)
    var = jnp.mean(yc * yc, axis=-1, keepdims=True)
    return _silu(yc * lax.rsqrt(var + EPS) * g + b)


CONV_HALO = 32
CONV_ROWS = 32


def _conv_prompt_kernel(a_ref, g_ref, w_ref, bdw_ref, gln_ref, bln_ref, yc_ref, st_ref, buf, shifted,
                        *, t_rows, nt):
    t = pl.program_id(1)

    @pl.when(t == 0)
    def _():
        buf[0:CONV_HALO, :] = jnp.zeros((CONV_HALO, CONV_CH), F32)

    @pl.when(t > 0)
    def _():
        buf[0:CONV_HALO, :] = buf[t_rows:t_rows + CONV_HALO, :]

    buf[CONV_HALO:CONV_HALO + t_rows, :] = a_ref[...] * _sigmoid(g_ref[...])
    off = CONV_HALO - (CONV_K - 1)
    span = t_rows + CONV_HALO - SUBLANES
    for r in range(1, SUBLANES):
        shifted[r - 1, 0:span, :] = buf[r:r + span, :]
    for blk in range(t_rows // CONV_ROWS):
        base = blk * CONV_ROWS
        acc = jnp.broadcast_to(bdw_ref[...], (CONV_ROWS, CONV_CH))
        for j in range(CONV_K):
            q, r = divmod(off + j, SUBLANES)
            lo = base + q * SUBLANES
            rows = buf[lo:lo + CONV_ROWS, :] if r == 0 else shifted[r - 1, lo:lo + CONV_ROWS, :]
            acc = acc + rows * w_ref[j:j + 1, :]
        yc_ref[base:base + CONV_ROWS, :] = _ln_silu(acc, gln_ref[...], bln_ref[...]).astype(BF16)

    @pl.when(t == nt - 1)
    def _():
        st_ref[...] = buf[t_rows + off:t_rows + CONV_HALO, :]


def _conv_prompt(z, w_dw, b_dw, g_ln, b_ln, batch, seq):
    t_rows = _pick(seq, 256, CONV_ROWS)
    nt = seq // t_rows
    mp = batch * seq
    row = lambda b, t: (0, 0)
    return pl.pallas_call(
        functools.partial(_conv_prompt_kernel, t_rows=t_rows, nt=nt),
        grid=(batch, nt),
        in_specs=[pl.BlockSpec((t_rows, CONV_CH), lambda b, t: (b * nt + t, Z_CONV_A // CONV_CH)),
                  pl.BlockSpec((t_rows, CONV_CH), lambda b, t: (b * nt + t, Z_CONV_G // CONV_CH)),
                  pl.BlockSpec((CONV_HALO, CONV_CH), row),
                  pl.BlockSpec((1, CONV_CH), row), pl.BlockSpec((1, CONV_CH), row), pl.BlockSpec((1, CONV_CH), row)],
        out_specs=[pl.BlockSpec((t_rows, CONV_CH), lambda b, t: (b * nt + t, 0)),
                   pl.BlockSpec((None, CONV_K - 1, CONV_CH), lambda b, t: (b, 0, 0))],
        out_shape=[jax.ShapeDtypeStruct((mp, CONV_CH), BF16),
                   jax.ShapeDtypeStruct((batch, CONV_K - 1, CONV_CH), F32)],
        scratch_shapes=[pltpu.VMEM((t_rows + CONV_HALO, CONV_CH), F32),
                        pltpu.VMEM((SUBLANES - 1, t_rows + CONV_HALO, CONV_CH), F32)],
        compiler_params=_params("parallel", "arbitrary"),
        name="conv_prompt",
    )(z, z, w_dw, b_dw, g_ln, b_ln)


def _conv_sample_kernel(a_ref, g_ref, st_ref, w_ref, bdw_ref, gln_ref, bln_ref, yc_ref, nst_ref, buf, ybuf,
                        *, n_seq, dec_seq):
    hist = CONV_K - 1
    a = a_ref[...] * _sigmoid(g_ref[...])
    w = w_ref[0:CONV_K, :]
    for s in range(n_seq):
        buf[s, 0:hist, :] = st_ref[s]
        buf[s, hist:hist + dec_seq, :] = a[s * dec_seq:(s + 1) * dec_seq, :]
    for s in range(n_seq):
        for t in range(dec_seq):
            y = jnp.sum(buf[s, t:t + CONV_K, :] * w, axis=0, keepdims=True) + bdw_ref[...]
            ybuf[s * dec_seq + t: s * dec_seq + t + 1, :] = y
        nst_ref[s] = buf[s, dec_seq:dec_seq + hist, :]
    yc_ref[...] = _ln_silu(ybuf[...], gln_ref[...], bln_ref[...]).astype(BF16)


def _conv_sample(layer, z, state_conv, w_dw, b_dw, g_ln, b_ln, mp, bs, dec_seq):
    n_seq = _pick(bs, 8, 1)
    rows = n_seq * dec_seq
    assert rows % 16 == 0 and mp % rows == 0
    hist = CONV_K - 1
    row = lambda i: (0, 0)
    return pl.pallas_call(
        functools.partial(_conv_sample_kernel, n_seq=n_seq, dec_seq=dec_seq),
        grid=(bs // n_seq,),
        in_specs=[pl.BlockSpec((rows, CONV_CH), lambda i: (mp // rows + i, Z_CONV_A // CONV_CH)),
                  pl.BlockSpec((rows, CONV_CH), lambda i: (mp // rows + i, Z_CONV_G // CONV_CH)),
                  pl.BlockSpec((None, n_seq, hist, CONV_CH), lambda i: (layer, i, 0, 0)),
                  pl.BlockSpec((CONV_HALO, CONV_CH), row),
                  pl.BlockSpec((1, CONV_CH), row), pl.BlockSpec((1, CONV_CH), row), pl.BlockSpec((1, CONV_CH), row)],
        out_specs=[pl.BlockSpec((rows, CONV_CH), lambda i: (i, 0)),
                   pl.BlockSpec((n_seq, hist, CONV_CH), lambda i: (i, 0, 0))],
        out_shape=[jax.ShapeDtypeStruct((bs * dec_seq, CONV_CH), BF16),
                   jax.ShapeDtypeStruct((bs, hist, CONV_CH), F32)],
        scratch_shapes=[pltpu.VMEM((n_seq, hist + SUBLANES + 2, CONV_CH), F32), pltpu.VMEM((rows, CONV_CH), F32)],
        compiler_params=_params("parallel"),
        name="conv_sample",
    )(z, z, state_conv, w_dw, b_dw, g_ln, b_ln)


def _hgrn_gates(hq, hf, hi, lb):
    log_lb = jnp.log(jnp.maximum(lb, LB_FLOOR))
    log_1m = jnp.log1p(-lb)
    log_sig = jnp.minimum(hf, 0.0) - jnp.log1p(jnp.exp(-jnp.abs(hf)))
    bb = log_1m + log_sig
    logf = jnp.maximum(log_lb, bb) + jnp.log1p(jnp.exp(-jnp.abs(log_lb - bb)))
    k = 1.0 - jnp.exp(logf)
    return _silu(hq), k, _silu(hi), logf


def _chunk_cumsum(bd, logf):
    out = None
    for part in _split3(logf):
        t = jnp.dot(bd, part, preferred_element_type=F32)
        out = t if out is None else out + t
    return out


def _intra_weights(q, k, cum, valid_fn, n_rows):
    lane = lax.broadcasted_iota(jnp.int32, (SUBLANES, LANES), 1)
    srow = lax.broadcasted_iota(jnp.int32, (SUBLANES, LANES), 0)
    blocks = [jnp.zeros((SUBLANES, LANES), F32) for _ in range(n_rows // SUBLANES)]
    for t in range(n_rows):
        qt = q[t:t + 1, :]
        ct = cum[t:t + 1, :]
        for rb in range(n_rows // SUBLANES):
            mask = valid_fn(srow + rb * SUBLANES, t, rb)
            if mask is None:
                continue
            sl = slice(rb * SUBLANES, (rb + 1) * SUBLANES)
            d = ct - cum[sl, :]
            if mask is not True:
                d = jnp.where(mask, d, NEG_BIG)
            w = jnp.sum(jnp.exp(d) * (qt * k[sl, :]), axis=-1, keepdims=True)
            blocks[rb] = jnp.where(lane == t, w, blocks[rb])
    return jnp.concatenate(blocks, axis=0)


def _chunk_attention(qc, kc, cc):
    n = qc.shape[0]
    nb = n // SUBLANES
    lane = lax.broadcasted_iota(jnp.int32, (SUBLANES, LANES), 1)
    srow = lax.broadcasted_iota(jnp.int32, (SUBLANES, LANES), 0)
    blocks = []
    for ib in range(nb):
        sl = slice(ib * SUBLANES, (ib + 1) * SUBLANES)
        cb, kb = cc[sl, :], kc[sl, :]
        blk = jnp.zeros((SUBLANES, LANES), F32)
        for tt in range(SUBLANES):
            t = ib * SUBLANES + tt
            d = jnp.where(srow <= tt, cc[t:t + 1, :] - cb, NEG_BIG)
            w = jnp.sum(jnp.exp(d) * (qc[t:t + 1, :] * kb), axis=-1, keepdims=True)
            blk = jnp.where(lane == t, w, blk)
        blocks.append(blk)
    att = jnp.concatenate(blocks, axis=0)
    if nb == 1:
        return att
    ks_parts, qs_parts = [], []
    for ib in range(1, nb):
        lo = ib * SUBLANES
        ref = cc[lo - 1:lo, :]
        kpart = kc[0:lo, :] * jnp.exp(ref - cc[0:lo, :])
        qpart = qc[lo:lo + SUBLANES, :] * jnp.exp(cc[lo:lo + SUBLANES, :] - ref)
        ks_parts.append(jnp.concatenate([kpart, jnp.zeros((n - lo, LANES), F32)], axis=0))
        qs_parts.append(jnp.concatenate([jnp.zeros((lo, LANES), F32), qpart,
                                         jnp.zeros((LANES - lo - SUBLANES, LANES), F32)], axis=0))
    ks = jnp.concatenate(ks_parts, axis=1).astype(BF16)
    qs = jnp.concatenate(qs_parts, axis=1).astype(BF16)
    return att + lax.dot_general(ks, qs, (((1,), (1,)), ((), ())), preferred_element_type=F32)


def _hgrn_prompt_kernel(hq_ref, hf_ref, hi_ref, hg_ref, lb_ref, go_ref, bd_ref, o_ref, s_ref, st_scr,
                        *, t_rows, nt, chunk, heads):
    t = pl.program_id(2)
    tn_dims = (((0,), (0,)), ((), ()))

    @pl.when(t == 0)
    def _():
        st_scr[...] = jnp.zeros(st_scr.shape, F32)

    q, k, v, logf = _hgrn_gates(hq_ref[...], hf_ref[...], hi_ref[...], lb_ref[...])
    cum = _chunk_cumsum(bd_ref[...], logf)
    sts = [st_scr[hd] for hd in range(heads)]
    outs = [[] for _ in range(heads)]
    for ci in range(t_rows // chunk):
        sl = slice(ci * chunk, (ci + 1) * chunk)
        for hd in range(heads):
            ln = slice(hd * HGRN_DK, (hd + 1) * HGRN_DK)
            qc, kc, cc = q[sl, ln], k[sl, ln], cum[sl, ln]
            vb = v[sl, ln].astype(BF16)
            att_t = _chunk_attention(qc, kc, cc)
            intra = lax.dot_general(att_t.astype(BF16), vb, tn_dims, preferred_element_type=F32)
            last = cc[chunk - 1:chunk, :]
            kv = lax.dot_general(vb, (kc * jnp.exp(last - cc)).astype(BF16), tn_dims, preferred_element_type=F32)
            inter = lax.dot_general((qc * jnp.exp(cc)).astype(BF16), sts[hd].astype(BF16),
                                    (((1,), (1,)), ((), ())), preferred_element_type=F32)
            outs[hd].append(inter + intra[0:chunk, :])
            sts[hd] = sts[hd] * jnp.exp(last) + kv
    hg = hg_ref[...]
    for hd in range(heads):
        ln = slice(hd * HGRN_DK, (hd + 1) * HGRN_DK)
        st_scr[hd] = sts[hd]
        o = jnp.concatenate(outs[hd], axis=0)
        o_ref[:, ln] = (_rms_rows(o, go_ref[...]) * _sigmoid(hg[:, ln])).astype(BF16)

    @pl.when(t == nt - 1)
    def _():
        for hd in range(heads):
            s_ref[hd] = st_scr[hd].T


def _block_tri(n, block):
    r = jnp.arange(n)
    return ((r[:, None] >= r[None, :]) & ((r[:, None] // block) == (r[None, :] // block))).astype(BF16)


def _hgrn_prompt(z, lb, g_o, batch, seq):
    chunk = HGRN_CHUNK if seq % HGRN_CHUNK == 0 else seq
    assert chunk % SUBLANES == 0 and chunk <= LANES
    t_rows = _pick(seq, 256, chunk)
    nt = seq // t_rows
    mp = batch * seq
    bd = _block_tri(t_rows, chunk)

    heads = 2
    hw = heads * HGRN_DK

    def zmap(off):
        return lambda b, h, t: (b * nt + t, off // hw + h)

    return pl.pallas_call(
        functools.partial(_hgrn_prompt_kernel, t_rows=t_rows, nt=nt, chunk=chunk, heads=heads),
        grid=(batch, HGRN_HEADS // heads, nt),
        in_specs=[pl.BlockSpec((t_rows, hw), zmap(Z_HQ)),
                  pl.BlockSpec((t_rows, hw), zmap(Z_HF)),
                  pl.BlockSpec((t_rows, hw), zmap(Z_HI)),
                  pl.BlockSpec((t_rows, hw), zmap(Z_HG)),
                  pl.BlockSpec((1, hw), lambda b, h, t: (0, h)),
                  pl.BlockSpec((1, HGRN_DV), lambda b, h, t: (0, 0)),
                  pl.BlockSpec((t_rows, t_rows), lambda b, h, t: (0, 0))],
        out_specs=[pl.BlockSpec((t_rows, hw), lambda b, h, t: (b * nt + t, h)),
                   pl.BlockSpec((None, heads, HGRN_DK, HGRN_DV), lambda b, h, t: (b, h, 0, 0))],
        out_shape=[jax.ShapeDtypeStruct((mp, HGRN_HEADS * HGRN_DV), BF16),
                   jax.ShapeDtypeStruct((batch, HGRN_HEADS, HGRN_DK, HGRN_DV), F32)],
        scratch_shapes=[pltpu.VMEM((heads, HGRN_DV, HGRN_DK), F32)],
        compiler_params=_params("parallel", "parallel", "arbitrary"),
        name="hgrn_prompt",
    )(z, z, z, z, lb, g_o, bd)


def _hgrn_sample_kernel(hq_ref, hf_ref, hi_ref, hg_ref, lb_ref, go_ref, bd_ref, s0_ref, o_ref, s_ref,
                        *, n_seq, dec_seq):
    rows = n_seq * dec_seq
    q, k, v, logf = _hgrn_gates(hq_ref[...], hf_ref[...], hi_ref[...], lb_ref[...])
    cum = _chunk_cumsum(bd_ref[...], logf)
    vb = v.astype(BF16)
    qe = (q * jnp.exp(cum)).astype(BF16)

    def valid(s_idx, t, rb):
        lo = (t // dec_seq) * dec_seq
        if lo // SUBLANES != rb:
            return None
        return jnp.logical_and(s_idx >= lo, s_idx <= t)

    att_t = _intra_weights(q, k, cum, valid, rows)
    intra = lax.dot_general(att_t.astype(BF16), vb, (((0,), (0,)), ((), ())),
                            preferred_element_type=F32)
    row = lax.broadcasted_iota(jnp.int32, (rows, HGRN_DK), 0)
    o = intra[0:rows, :]
    for s in range(n_seq):
        in_seq = (row // dec_seq) == s
        st = s0_ref[s].T
        inter = lax.dot_general(qe, st.astype(BF16), (((1,), (1,)), ((), ())), preferred_element_type=F32)
        o = o + jnp.where(in_seq, inter, 0.0)
        last = cum[(s + 1) * dec_seq - 1:(s + 1) * dec_seq, :]
        kt = (k * jnp.exp(jnp.where(in_seq, last - cum, NEG_BIG))).astype(BF16)
        st_new = st * jnp.exp(last) + lax.dot_general(vb, kt, (((0,), (0,)), ((), ())),
                                                      preferred_element_type=F32)
        s_ref[s] = st_new.T
    o_ref[...] = (_rms_rows(o, go_ref[...]) * _sigmoid(hg_ref[...])).astype(BF16)


def _hgrn_sample(layer, z, state_hgrn, lb, g_o, mp, bs, dec_seq):
    n_seq = _pick(bs, max(1, 16 // dec_seq), 1)
    rows = n_seq * dec_seq
    assert rows % 16 == 0 and mp % rows == 0 and SUBLANES % dec_seq == 0
    bd = _block_tri(rows, dec_seq)

    def zmap(off):
        return lambda i, h: (mp // rows + i, off // HGRN_DK + h)

    return pl.pallas_call(
        functools.partial(_hgrn_sample_kernel, n_seq=n_seq, dec_seq=dec_seq),
        grid=(bs // n_seq, HGRN_HEADS),
        in_specs=[pl.BlockSpec((rows, HGRN_DK), zmap(Z_HQ)),
                  pl.BlockSpec((rows, HGRN_DK), zmap(Z_HF)),
                  pl.BlockSpec((rows, HGRN_DK), zmap(Z_HI)),
                  pl.BlockSpec((rows, HGRN_DK), zmap(Z_HG)),
                  pl.BlockSpec((1, HGRN_DK), lambda i, h: (0, h)),
                  pl.BlockSpec((1, HGRN_DV), lambda i, h: (0, 0)),
                  pl.BlockSpec((rows, rows), lambda i, h: (0, 0)),
                  pl.BlockSpec((None, n_seq, None, HGRN_DK, HGRN_DV), lambda i, h: (layer, i, h, 0, 0))],
        out_specs=[pl.BlockSpec((rows, HGRN_DV), lambda i, h: (i, h)),
                   pl.BlockSpec((n_seq, None, HGRN_DK, HGRN_DV), lambda i, h: (i, h, 0, 0))],
        out_shape=[jax.ShapeDtypeStruct((bs * dec_seq, HGRN_HEADS * HGRN_DV), BF16),
                   jax.ShapeDtypeStruct((bs, HGRN_HEADS, HGRN_DK, HGRN_DV), F32)],
        compiler_params=_params("parallel", "parallel"),
        name="hgrn_sample",
    )(z, z, z, z, lb, g_o, bd, state_hgrn)


def _mix_kernel(om_ref, yc_ref, oh_ref, wm_ref, wc_ref, wh_ref, g0_ref, g1_ref, g2_ref, mix_ref):
    ym = jnp.dot(om_ref[...], wm_ref[...], preferred_element_type=F32)
    yc = jnp.dot(yc_ref[...], wc_ref[...], preferred_element_type=F32)
    yh = jnp.dot(oh_ref[...], wh_ref[...], preferred_element_type=F32)
    mix = _sigmoid(g0_ref[...]) * ym + _sigmoid(g1_ref[...]) * yc + _sigmoid(g2_ref[...]) * yh
    mix_ref[...] = mix.astype(BF16)


def _mix(o_mla, yc, o_h, w_o_mla, w_pw, w_o_hgrn, z):
    m = o_mla.shape[0]
    bm = _pick(m, 768, 16)
    bn = 512

    def gmap(off):
        return lambda i, j: (i, off // bn + j)

    return pl.pallas_call(
        _mix_kernel,
        grid=(m // bm, D_MODEL // bn),
        in_specs=[pl.BlockSpec((bm, MLA_HEADS * V_DIM), lambda i, j: (i, 0)),
                  pl.BlockSpec((bm, CONV_CH), lambda i, j: (i, 0)),
                  pl.BlockSpec((bm, HGRN_HEADS * HGRN_DV), lambda i, j: (i, 0)),
                  pl.BlockSpec((MLA_HEADS * V_DIM, bn), lambda i, j: (0, j)),
                  pl.BlockSpec((CONV_CH, bn), lambda i, j: (0, j)),
                  pl.BlockSpec((HGRN_HEADS * HGRN_DV, bn), lambda i, j: (0, j)),
                  pl.BlockSpec((bm, bn), gmap(Z_G0)),
                  pl.BlockSpec((bm, bn), gmap(Z_G1)),
                  pl.BlockSpec((bm, bn), gmap(Z_G2))],
        out_specs=pl.BlockSpec((bm, bn), lambda i, j: (i, j)),
        out_shape=jax.ShapeDtypeStruct((m, D_MODEL), BF16),
        compiler_params=_params("parallel", "arbitrary"),
        name="branch_mix",
    )(o_mla, yc, o_h, w_o_mla, w_pw, w_o_hgrn, z, z, z)


def _resmm_kernel(a_ref, w_ref, x_ref, o_ref):
    o_ref[...] = x_ref[...] + jnp.dot(a_ref[...], w_ref[...], preferred_element_type=F32)


def _residual_matmul(a, w, x, bm_target, bn):
    m, k = a.shape
    n = w.shape[1]
    bm = _pick(m, bm_target, 16)
    return pl.pallas_call(
        _resmm_kernel,
        grid=(m // bm, n // bn),
        in_specs=[pl.BlockSpec((bm, k), lambda i, j: (i, 0)),
                  pl.BlockSpec((k, bn), lambda i, j: (0, j)),
                  pl.BlockSpec((bm, bn), lambda i, j: (i, j))],
        out_specs=pl.BlockSpec((bm, bn), lambda i, j: (i, j)),
        out_shape=jax.ShapeDtypeStruct((m, n), F32),
        compiler_params=_params("parallel", "arbitrary"),
        name="residual_matmul",
    )(a, w, x)


def _ffn_up_kernel(x_ref, g_ref, wg_ref, wu_ref, o_ref, h_scr):
    @pl.when(pl.program_id(1) == 0)
    def _():
        h_scr[...] = _rms_rows(x_ref[...], g_ref[...]).astype(BF16)

    h = h_scr[...]
    gate = jnp.dot(h, wg_ref[...], preferred_element_type=F32)
    up = jnp.dot(h, wu_ref[...], preferred_element_type=F32)
    o_ref[...] = (_silu(gate) * up).astype(BF16)


def _ffn_up(x, g, w_gu):
    m, k = x.shape
    bm = _pick(m, 1536, 16)
    bn = 512
    nj = D_FF // bn
    return pl.pallas_call(
        _ffn_up_kernel,
        grid=(m // bm, nj),
        in_specs=[pl.BlockSpec((bm, k), lambda i, j: (i, 0)),
                  pl.BlockSpec((1, k), lambda i, j: (0, 0)),
                  pl.BlockSpec((k, bn), lambda i, j: (0, j)),
                  pl.BlockSpec((k, bn), lambda i, j: (0, nj + j))],
        out_specs=pl.BlockSpec((bm, bn), lambda i, j: (i, j)),
        out_shape=jax.ShapeDtypeStruct((m, D_FF), BF16),
        scratch_shapes=[pltpu.VMEM((bm, k), BF16)],
        compiler_params=_params("parallel", "arbitrary"),
        name="ffn_up",
    )(x, g, w_gu, w_gu)


def _ple_kernel(x_ref, g_ref, wg_ref, p_ref, wp_ref, xr_ref, o_ref, h_scr, p_scr):
    @pl.when(pl.program_id(1) == 0)
    def _():
        h_scr[...] = _rms_rows(x_ref[...], g_ref[...]).astype(BF16)
        p_scr[...] = p_ref[...].astype(BF16)

    gate = _sigmoid(jnp.dot(h_scr[...], wg_ref[...], preferred_element_type=F32))
    proj = jnp.dot(p_scr[...], wp_ref[...], preferred_element_type=F32)
    o_ref[...] = xr_ref[...] + proj * gate


def _ple(x, g, w_gate, p, w_proj):
    m, k = x.shape
    bm = _pick(m, 768, 16)
    bn = 512
    return pl.pallas_call(
        _ple_kernel,
        grid=(m // bm, D_MODEL // bn),
        in_specs=[pl.BlockSpec((bm, k), lambda i, j: (i, 0)),
                  pl.BlockSpec((1, k), lambda i, j: (0, 0)),
                  pl.BlockSpec((k, bn), lambda i, j: (0, j)),
                  pl.BlockSpec((bm, PLE_DIM), lambda i, j: (i, 0)),
                  pl.BlockSpec((PLE_DIM, bn), lambda i, j: (0, j)),
                  pl.BlockSpec((bm, bn), lambda i, j: (i, j))],
        out_specs=pl.BlockSpec((bm, bn), lambda i, j: (i, j)),
        out_shape=jax.ShapeDtypeStruct((m, D_MODEL), F32),
        scratch_shapes=[pltpu.VMEM((bm, k), BF16), pltpu.VMEM((bm, PLE_DIM), BF16)],
        compiler_params=_params("parallel", "arbitrary"),
        name="ple",
    )(x, g, w_gate, p, w_proj, x)


def _rope_tables(pos):
    half = ROPE_DIM // 2
    inv = ROPE_THETA ** (-jnp.arange(half, dtype=F32) / half)
    ang = pos.astype(F32)[:, None] * inv[None, :]
    cos, sin = jnp.cos(ang), jnp.sin(ang)
    reps = LANES // ROPE_DIM
    cos_t = jnp.tile(jnp.concatenate([cos, cos], axis=-1), (1, reps))
    sin_t = jnp.tile(jnp.concatenate([-sin, sin], axis=-1), (1, reps))
    return cos_t, sin_t


def _row(v):
    return v.reshape(1, -1).astype(F32)


def kernel(x_prompt, x_sample, cache_latent, cache_kscale, state_conv, state_hgrn, page_table, p_prompt, p_sample, g_mix, w_in, g_q, w_uq, g_kv, g_qk, g_kk, w_uk, w_uv, w_o_mla, w_dw, b_dw, g_cln, b_cln, w_pw, lb_logits, g_hgrn, w_o_hgrn, w_out, g_ffn, w_gu, w_down, g_ple, w_ple_gate, w_ple_proj):
    depth = w_in.shape[0]
    batch, seq, _ = x_prompt.shape
    bs, dec_seq, _ = x_sample.shape
    mp, ms = batch * seq, bs * dec_seq
    n_pages, page = page_table.shape[1], cache_latent.shape[2]
    past = n_pages * page

    lb_p = jax.nn.softmax(lb_logits.astype(F32), axis=0)
    lb_all = jnp.cumsum(lb_p, axis=0) - lb_p[0:1]

    cos_p, sin_p = _rope_tables(jnp.arange(seq, dtype=jnp.int32))
    cos_s, sin_s = _rope_tables(past + jnp.arange(dec_seq, dtype=jnp.int32))
    cos_t = jnp.concatenate([jnp.tile(cos_p, (batch, 1)), jnp.tile(cos_s, (bs, 1))], axis=0)
    sin_t = jnp.concatenate([jnp.tile(sin_p, (batch, 1)), jnp.tile(sin_s, (bs, 1))], axis=0)

    rows_q = MLA_HEADS * dec_seq
    e_sel = (jnp.arange(rows_q)[:, None] // dec_seq == jnp.arange(MLA_HEADS)[None, :]).astype(BF16)
    pages_per_step = _pick(n_pages, 32, 1)
    cache_latent_t = jnp.swapaxes(cache_latent, 2, 3)
    cache_kscale_t = jnp.swapaxes(cache_kscale, 2, 3)

    x = jnp.concatenate([x_prompt.reshape(mp, D_MODEL), x_sample.reshape(ms, D_MODEL)], axis=0)
    lat_p, ks_p, conv_p, hg_p, lat_s, ks_s, conv_s, hg_s = ([] for _ in range(8))
    for i in range(depth):
        wi = w_in[i]
        w_in_r = jnp.concatenate(
            [wi[:, 0:Q_LORA + KV_LORA], wi[:, Q_LORA + KV_LORA + ROPE_DIM:], wi[:, Q_LORA + KV_LORA:Q_LORA + KV_LORA + ROPE_DIM],
             jnp.zeros((D_MODEL, Z_COLS - wi.shape[1]), wi.dtype)], axis=1).astype(BF16)
        wq = w_uq[i].reshape(Q_LORA, MLA_HEADS, QK_DIM)
        w_uq_r = jnp.concatenate([wq[:, :, 0:NOPE_DIM].reshape(Q_LORA, -1), wq[:, :, NOPE_DIM:].reshape(Q_LORA, -1)],
                                 axis=1).astype(BF16)
        w_uk2 = w_uk[i].reshape(KV_LORA, -1).astype(BF16)
        w_uv2 = w_uv[i].reshape(KV_LORA, -1).astype(BF16)
        w_uk_t = jnp.transpose(w_uk[i], (1, 2, 0)).astype(BF16)
        w_uv_h = jnp.transpose(w_uv[i], (1, 0, 2)).astype(BF16)
        gqkn = _row(g_qk[i, 0:NOPE_DIM])
        gkkn = _row(g_kk[i, 0:NOPE_DIM])
        gqkr = _row(jnp.tile(g_qk[i, NOPE_DIM:], LANES // ROPE_DIM))
        gkkr = _row(jnp.concatenate([g_kk[i, NOPE_DIM:], jnp.zeros((LANES - ROPE_DIM,), F32)]))
        w_dw_p = jnp.concatenate([w_dw[i], jnp.zeros((CONV_HALO - CONV_K, CONV_CH), F32)], axis=0)
        lb = _row(lb_all[i])

        z = _inproj(x, _row(g_mix[i]), w_in_r)
        q_all = _qproj(z, _row(g_q[i]), w_uq_r, gqkn, gkkn, gqkr, cos_t, sin_t)
        rows, kscale, k_all, v_all = _kvproj(z, _row(g_kv[i]), gkkr, w_uk2, w_uv2, cos_t, sin_t)

        o_mla_p = _flash(q_all, k_all, v_all, batch, seq)
        q_s = _absorb(q_all, w_uk_t, mp, ms)
        q_s = q_s.reshape(MLA_HEADS, bs, dec_seq, -1).transpose(1, 0, 2, 3).reshape(bs, rows_q, -1)
        new_rows = rows[mp:].reshape(bs, dec_seq, -1)
        new_ks = kscale[mp:].reshape(bs, dec_seq, MLA_HEADS)
        o_lat = _paged_attention(i, page_table, q_s, e_sel, cache_latent_t, cache_kscale_t, new_rows, new_ks,
                                 pages_per_step)
        o_lat_h = o_lat.reshape(bs, MLA_HEADS, dec_seq, KV_LORA).transpose(1, 0, 2, 3)
        o_lat_h = o_lat_h.reshape(MLA_HEADS, ms, KV_LORA).astype(BF16)
        o_mla_s = _sample_vproj(o_lat_h, w_uv_h)
        o_mla = jnp.concatenate([o_mla_p, o_mla_s], axis=0)

        yc_p, conv_new_p = _conv_prompt(z, w_dw_p, _row(b_dw[i]), _row(g_cln[i]), _row(b_cln[i]), batch, seq)
        yc_s, conv_new_s = _conv_sample(i, z, state_conv, w_dw_p, _row(b_dw[i]), _row(g_cln[i]), _row(b_cln[i]),
                                        mp, bs, dec_seq)
        yc = jnp.concatenate([yc_p, yc_s], axis=0)

        oh_p, s_new_p = _hgrn_prompt(z, lb, _row(g_hgrn[i]), batch, seq)
        oh_s, s_new_s = _hgrn_sample(i, z, state_hgrn, lb, _row(g_hgrn[i]), mp, bs, dec_seq)
        o_h = jnp.concatenate([oh_p, oh_s], axis=0)

        mix = _mix(o_mla, yc, o_h, w_o_mla[i].astype(BF16), w_pw[i].astype(BF16), w_o_hgrn[i].astype(BF16), z)
        x = _residual_matmul(mix, w_out[i].astype(BF16), x, 1536, 512)

        act = _ffn_up(x, _row(g_ffn[i]), w_gu[i].astype(BF16))
        x = _residual_matmul(act, w_down[i].astype(BF16), x, 768, 512)
        p_emb = jnp.concatenate([p_prompt[i].reshape(mp, PLE_DIM), p_sample[i].reshape(ms, PLE_DIM)], axis=0)
        x = _ple(x, _row(g_ple[i]), w_ple_gate[i].astype(BF16), p_emb, w_ple_proj[i].astype(BF16))

        lat_p.append(rows[:mp].reshape(batch, seq, -1))
        ks_p.append(kscale[:mp].reshape(batch, seq, MLA_HEADS))
        conv_p.append(conv_new_p)
        hg_p.append(s_new_p)
        lat_s.append(new_rows)
        ks_s.append(new_ks)
        conv_s.append(conv_new_s)
        hg_s.append(s_new_s)

    return (x[:mp].reshape(batch, seq, D_MODEL), x[mp:].reshape(bs, dec_seq, D_MODEL),
            jnp.stack(lat_p), jnp.stack(ks_p), jnp.stack(conv_p), jnp.stack(hg_p),
            jnp.stack(lat_s), jnp.stack(ks_s), jnp.stack(conv_s), jnp.stack(hg_s))
```
